```python
import math
import jax, jax.numpy as jnp
from jax import lax
import numpy as np

D_MODEL = 2048
BATCH = 2
SEQ = 4096
DEPTH = 2
DEC_BATCH = 128
DEC_SEQ = 1
PAST_LEN = 2048
PAGE_SIZE = 128

N_AB = (DEPTH + 1) // 2
N_C = DEPTH // 2
MIX_A = D_MODEL // 2
A_GROUPS = 4
A_GROUP_DIM = MIX_A // A_GROUPS
CHUNK = 128
MIX_B = D_MODEL // 2
B_HEADS = 8
B_QK_DIM = 64
B_K_ROW = 2 * B_QK_DIM
B_V_DIM = MIX_B // B_HEADS
QK_W = B_HEADS * 2 * B_QK_DIM
B_BLOCK = 128
ATTN_SCALE = 1.0 / math.sqrt(B_QK_DIM)
AB_IN = 3 * MIX_A + 2 * QK_W + 2 * MIX_B
C_WIDTH = D_MODEL
C_WINDOWS = (2, 4, 8, 16)
C_GROUPS = len(C_WINDOWS)
C_GROUP_DIM = C_WIDTH // C_GROUPS
C_HIST = max(C_WINDOWS) - 1
C_IN = 2 * C_WIDTH
ALPHA = (2 * DEPTH) ** 0.25
BETA = (8 * DEPTH) ** -0.25
LN_EPS = 1e-5
NEG = -1e30

kernel_name = "hybrid_gmlp_diffattn_pool_decode_step"


def layer_norm(x, g, b):
    xf = x.astype(jnp.float32)
    mu = jnp.mean(xf, axis=-1, keepdims=True)
    var = jnp.mean(jnp.square(xf - mu), axis=-1, keepdims=True)
    return ((xf - mu) * lax.rsqrt(var + LN_EPS) * g.astype(jnp.float32) + b.astype(jnp.float32)).astype(x.dtype)


def rms_norm(x, g):
    xf = x.astype(jnp.float32)
    ms = jnp.mean(jnp.square(xf), axis=-1, keepdims=True)
    return (xf * lax.rsqrt(ms + LN_EPS) * g.astype(jnp.float32)).astype(x.dtype)


def chunk_spatial_gate(u, v, w_s, b_s):
    bsz, t, _ = v.shape
    n_chunks = -(-t // CHUNK)
    pad = n_chunks * CHUNK - t
    vb = jnp.pad(v, ((0, 0), (0, pad), (0, 0))).reshape(bsz, n_chunks, CHUNK, A_GROUPS, A_GROUP_DIM)
    mask = jnp.tril(jnp.ones((CHUNK, CHUNK), dtype=bool))
    w = w_s * mask[None].astype(w_s.dtype)
    mixed = jnp.einsum('gij,bcjgd->bcigd', w, vb) + b_s.T[None, None, :, :, None]
    mixed = mixed.reshape(bsz, n_chunks * CHUNK, MIX_A)[:, :t]
    return u * mixed


def diff_weights(s, mask, lam):
    s = jnp.where(mask, s, NEG)
    p = jax.nn.softmax(s, axis=-1)
    return p[:, 0] - lam * p[:, 1]


def prompt_diff_attention(q, k, v, lam):
    bsz, t = q.shape[:2]
    nb = t // B_BLOCK
    q_blocks = q.reshape(bsz, nb, B_BLOCK, B_HEADS, 2, B_QK_DIM).transpose(1, 0, 2, 3, 4, 5)
    k_pos = jnp.arange(t)

    def block(args):
        qb, i = args
        q_pos = i * B_BLOCK + jnp.arange(B_BLOCK)
        s = jnp.einsum('bqhsd,bkhsd->bshqk', qb, k).astype(jnp.float32)
        a = diff_weights(s, k_pos[None, :] <= q_pos[:, None], lam)
        return jnp.einsum('bhqk,bkhd->bqhd', a.astype(v.dtype), v)

    o = lax.map(block, (q_blocks, jnp.arange(nb)))
    return o.transpose(1, 0, 2, 3, 4).reshape(bsz, t, B_HEADS, B_V_DIM)


def sample_diff_attention(q, k_new, v_new, k_past, v_past, lam):
    bsz, t = q.shape[:2]
    p = k_past.shape[1]
    kp = k_past.reshape(bsz, p, B_HEADS, 2, B_QK_DIM)
    s_past = jnp.einsum('bqhsd,bkhsd->bshqk', q, kp)
    s_new = jnp.einsum('bqhsd,bkhsd->bshqk', q, k_new)
    s = jnp.concatenate([s_past, s_new], axis=-1).astype(jnp.float32)
    mask = jnp.concatenate([jnp.ones((t, p), dtype=bool), jnp.tril(jnp.ones((t, t), dtype=bool))], axis=-1)
    a = diff_weights(s, mask, lam).astype(v_new.dtype)
    return (jnp.einsum('bhqk,bkhd->bqhd', a[..., :p], v_past)
            + jnp.einsum('bhqk,bkhd->bqhd', a[..., p:], v_new))


def ab_mixer(x, w_in, a_ln_g, a_ln_b, a_w_s, a_b_s, lam, lam_init, subln_g, w_out, k_past, v_past):
    bsz, t, _ = x.shape
    h = jnp.einsum('btd,de->bte', x, w_in)
    a_u, a_v, a_gate, q, k, v, b_gate = jnp.split(
        h, [MIX_A, 2 * MIX_A, 3 * MIX_A, 3 * MIX_A + QK_W, 3 * MIX_A + 2 * QK_W, 3 * MIX_A + 2 * QK_W + MIX_B], axis=-1)
    a_v = layer_norm(a_v, a_ln_g, a_ln_b)
    a_out = chunk_spatial_gate(a_u, a_v, a_w_s, a_b_s) * jax.nn.silu(a_gate)
    q = q.reshape(bsz, t, B_HEADS, 2, B_QK_DIM) * ATTN_SCALE
    k_rows = k.reshape(bsz, t, B_HEADS, B_K_ROW)
    v_rows = v.reshape(bsz, t, B_HEADS, B_V_DIM)
    k_new = k_rows.reshape(bsz, t, B_HEADS, 2, B_QK_DIM)
    if k_past is None:
        o = prompt_diff_attention(q, k_new, v_rows, lam)
    else:
        o = sample_diff_attention(q, k_new, v_rows, k_past, v_past, lam)
    o = rms_norm(o, subln_g) * (1.0 - lam_init)
    b_out = o.reshape(bsz, t, MIX_B) * jax.nn.silu(b_gate)
    out = jnp.einsum('bte,ed->btd', jnp.concatenate([a_out, b_out], axis=-1), w_out)
    return out, k_rows, v_rows, a_v


def pool_mixer(x, hist, start_pos, w_in, w_grp, b_grp, scale, w_out):
    bsz, t, _ = x.shape
    h = jnp.einsum('btd,de->bte', x, w_in)
    hp, gate = jnp.split(h, 2, axis=-1)
    ext = jnp.concatenate([hist.astype(hp.dtype), hp], axis=1)
    cs = jnp.concatenate([jnp.zeros((bsz, 1, C_WIDTH), jnp.float32),
                          jnp.cumsum(ext.astype(jnp.float32), axis=1)], axis=1)
    pos = start_pos + jnp.arange(t)
    means = []
    for g, w in enumerate(C_WINDOWS):
        sl = slice(g * C_GROUP_DIM, (g + 1) * C_GROUP_DIM)
        win_sum = cs[:, C_HIST + 1:C_HIST + 1 + t, sl] - cs[:, C_HIST + 1 - w:C_HIST + 1 - w + t, sl]
        count = jnp.minimum(pos + 1, w).astype(jnp.float32)
        means.append(win_sum / count[None, :, None])
    pooled = (jnp.concatenate(means, axis=-1) - hp.astype(jnp.float32)).astype(hp.dtype)
    mixed = jnp.einsum('btgc,gce->btge', pooled.reshape(bsz, t, C_GROUPS, C_GROUP_DIM), w_grp)
    mixed = mixed.reshape(bsz, t, C_WIDTH) + b_grp
    out = jnp.einsum('btc,cd->btd', mixed * scale * jax.nn.silu(gate), w_out)
    return out, ext[:, t:]


def setup_inputs(seed: int = 0) -> dict:
    key = jax.random.key(seed)
    ks = jax.random.split(key, 32)
    n_pages = PAST_LEN // PAGE_SIZE
    n_pool = (DEC_BATCH * n_pages * 5) // 4

    def nrm(k, shape, s):
        return jax.random.normal(k, shape, jnp.float32) * s

    x_prompt = nrm(ks[0], (BATCH, SEQ, D_MODEL), 1.0)
    x_sample = nrm(ks[1], (DEC_BATCH, DEC_SEQ, D_MODEL), 1.0)
    cache_k = nrm(ks[2], (N_AB, n_pool, PAGE_SIZE, B_HEADS, B_K_ROW), 1.0)
    cache_v = nrm(ks[3], (N_AB, n_pool, PAGE_SIZE, B_HEADS, B_V_DIM), 1.0)
    state_pool = nrm(ks[4], (N_C, DEC_BATCH, C_HIST, C_WIDTH), 1.0)
    page_table = jax.random.permutation(ks[5], n_pool)[:DEC_BATCH * n_pages].reshape(DEC_BATCH, n_pages).astype(jnp.int32)
    ln_g = 1.0 + nrm(ks[6], (DEPTH, D_MODEL), 0.02)
    ln_b = nrm(ks[7], (DEPTH, D_MODEL), 0.02)
    w_in_ab = nrm(ks[8], (N_AB, D_MODEL, AB_IN), D_MODEL ** -0.5)
    a_ln_g = 1.0 + nrm(ks[9], (N_AB, MIX_A), 0.02)
    a_ln_b = nrm(ks[10], (N_AB, MIX_A), 0.02)
    a_w_s = nrm(ks[11], (N_AB, A_GROUPS, CHUNK, CHUNK), CHUNK ** -0.5)
    a_b_s = 1.0 + nrm(ks[12], (N_AB, A_GROUPS, CHUNK), 0.02)
    b_lq1 = nrm(ks[13], (N_AB, B_QK_DIM), 0.1)
    b_lk1 = nrm(ks[14], (N_AB, B_QK_DIM), 0.1)
    b_lq2 = nrm(ks[15], (N_AB, B_QK_DIM), 0.1)
    b_lk2 = nrm(ks[16], (N_AB, B_QK_DIM), 0.1)
    b_subln_g = 1.0 + nrm(ks[17], (N_AB, B_V_DIM), 0.02)
    w_out_ab = nrm(ks[18], (N_AB, MIX_A + MIX_B, D_MODEL), BETA * (MIX_A + MIX_B) ** -0.5)
    w_in_c = nrm(ks[19], (N_C, D_MODEL, C_IN), D_MODEL ** -0.5)
    c_w_grp = nrm(ks[20], (N_C, C_GROUPS, C_GROUP_DIM, C_GROUP_DIM), C_GROUP_DIM ** -0.5)
    c_b_grp = nrm(ks[21], (N_C, C_WIDTH), 0.02)
    c_scale = 1.0 + nrm(ks[22], (N_C, C_WIDTH), 0.02)
    w_out_c = nrm(ks[23], (N_C, C_WIDTH, D_MODEL), BETA * C_WIDTH ** -0.5)
    return {"x_prompt": x_prompt, "x_sample": x_sample, "cache_k": cache_k, "cache_v": cache_v,
            "state_pool": state_pool, "page_table": page_table, "ln_g": ln_g, "ln_b": ln_b,
            "w_in_ab": w_in_ab, "a_ln_g": a_ln_g, "a_ln_b": a_ln_b, "a_w_s": a_w_s, "a_b_s": a_b_s,
            "b_lq1": b_lq1, "b_lk1": b_lk1, "b_lq2": b_lq2, "b_lk2": b_lk2, "b_subln_g": b_subln_g,
            "w_out_ab": w_out_ab, "w_in_c": w_in_c, "c_w_grp": c_w_grp, "c_b_grp": c_b_grp,
            "c_scale": c_scale, "w_out_c": w_out_c}


def reference(x_prompt, x_sample, cache_k, cache_v, state_pool, page_table, ln_g, ln_b,
              w_in_ab, a_ln_g, a_ln_b, a_w_s, a_b_s, b_lq1, b_lk1, b_lq2, b_lk2, b_subln_g,
              w_out_ab, w_in_c, c_w_grp, c_b_grp, c_scale, w_out_c):
    n_pages = page_table.shape[1]
    past_len = n_pages * cache_k.shape[2]
    dec_b = x_sample.shape[0]
    yp, ys = x_prompt, x_sample
    k_p_list, v_p_list, k_s_list, v_s_list, cv_s_list = [], [], [], [], []
    pool_p_list, pool_s_list = [], []
    for layer in range(DEPTH):
        i = layer // 2
        if layer % 2 == 0:
            lam_init = 0.8 - 0.6 * math.exp(-0.3 * layer)
            lam = (jnp.exp(jnp.sum(b_lq1[i].astype(jnp.float32) * b_lk1[i].astype(jnp.float32)))
                   - jnp.exp(jnp.sum(b_lq2[i].astype(jnp.float32) * b_lk2[i].astype(jnp.float32)))
                   + lam_init)
            out_p, k_p, v_p, _ = ab_mixer(yp, w_in_ab[i], a_ln_g[i], a_ln_b[i], a_w_s[i], a_b_s[i],
                                          lam, lam_init, b_subln_g[i], w_out_ab[i], None, None)
            k_past = cache_k[i, page_table].reshape(dec_b, past_len, B_HEADS, B_K_ROW)
            v_past = cache_v[i, page_table].reshape(dec_b, past_len, B_HEADS, B_V_DIM)
            out_s, k_s, v_s, av_s = ab_mixer(ys, w_in_ab[i], a_ln_g[i], a_ln_b[i], a_w_s[i], a_b_s[i],
                                             lam, lam_init, b_subln_g[i], w_out_ab[i], k_past, v_past)
            k_p_list.append(k_p)
            v_p_list.append(v_p)
            k_s_list.append(k_s)
            v_s_list.append(v_s)
            cv_s_list.append(av_s)
        else:
            hist_p = jnp.zeros((yp.shape[0], C_HIST, C_WIDTH), yp.dtype)
            out_p, hp_new = pool_mixer(yp, hist_p, 0, w_in_c[i], c_w_grp[i], c_b_grp[i], c_scale[i], w_out_c[i])
            out_s, hs_new = pool_mixer(ys, state_pool[i], past_len, w_in_c[i], c_w_grp[i], c_b_grp[i],
                                       c_scale[i], w_out_c[i])
            pool_p_list.append(hp_new)
            pool_s_list.append(hs_new)
        yp = layer_norm(ALPHA * yp + out_p, ln_g[layer], ln_b[layer])
        ys = layer_norm(ALPHA * ys + out_s, ln_g[layer], ln_b[layer])
    k_prompt_new = jnp.stack(k_p_list)
    v_prompt_new = jnp.stack(v_p_list)
    k_sample_new = jnp.stack(k_s_list)
    v_sample_new = jnp.stack(v_s_list)
    chunk_v_sample = jnp.stack(cv_s_list)
    pool_prompt_new = jnp.stack(pool_p_list)
    pool_sample_new = jnp.stack(pool_s_list)
    return (yp, ys, k_prompt_new, v_prompt_new, k_sample_new, v_sample_new, chunk_v_sample,
            pool_prompt_new, pool_sample_new)
```

```python
import functools
import math

import jax
import jax.numpy as jnp
from jax import lax
from jax.experimental import pallas as pl
from jax.experimental.pallas import tpu as pltpu

D_MODEL = 2048
DEPTH = 2
PAGE_SIZE = 128
MIX_A = D_MODEL // 2
A_GROUPS = 4
A_GROUP_DIM = MIX_A // A_GROUPS
CHUNK = 128
MIX_B = D_MODEL // 2
B_HEADS = 8
B_QK_DIM = 64
B_K_ROW = 2 * B_QK_DIM
B_V_DIM = MIX_B // B_HEADS
QK_W = B_HEADS * 2 * B_QK_DIM
ATTN_SCALE = 1.0 / math.sqrt(B_QK_DIM)
AB_IN = 3 * MIX_A + 2 * QK_W + 2 * MIX_B
C_WIDTH = D_MODEL
C_WINDOWS = (2, 4, 8, 16)
C_GROUPS = len(C_WINDOWS)
C_GROUP_DIM = C_WIDTH // C_GROUPS
C_HIST = max(C_WINDOWS) - 1
ALPHA = (2 * DEPTH) ** 0.25
LN_EPS = 1e-5
NEG = -1e30
LAM_INIT_0 = 0.8 - 0.6 * math.exp(-0.3 * 0)

V7X_SUBLANES = 8
V7X_LANES = 128
V7X_VMEM_LIMIT_BYTES = 56 * 1024 * 1024

TM_PROJ = 256
TQ = 512
HIST_PAD = 16
POOL_BB = 16

BF16 = jnp.bfloat16
F32 = jnp.float32


def _dot(a, b):
    return jnp.dot(a, b, preferred_element_type=F32)


def _dot_nt(a, b):
    return lax.dot_general(a, b, (((1,), (1,)), ((), ())), preferred_element_type=F32)


def _layer_norm(x, g, b):
    mu = jnp.mean(x, axis=-1, keepdims=True)
    xc = x - mu
    var = jnp.mean(xc * xc, axis=-1, keepdims=True)
    return xc * lax.rsqrt(var + LN_EPS) * g + b


def _silu(x):
    return x / (1.0 + jnp.exp(-x))


def _lam(lq1, lk1, lq2, lk2):
    s1 = jnp.sum(lq1 * lk1, axis=-1, keepdims=True)
    s2 = jnp.sum(lq2 * lk2, axis=-1, keepdims=True)
    return jnp.exp(s1) - jnp.exp(s2) + LAM_INIT_0


def _sub_ln(o, g):
    ms = jnp.mean(o * o, axis=-1, keepdims=True)
    return o * lax.rsqrt(ms + LN_EPS) * g * (1.0 - LAM_INIT_0)


def _params(semantics):
    return pltpu.CompilerParams(dimension_semantics=semantics,
                                vmem_limit_bytes=V7X_VMEM_LIMIT_BYTES)


def _resident(shape, index):
    return pl.BlockSpec(shape, lambda *_: index, pipeline_mode=pl.Buffered(1))


def _proj_a_kernel(x_ref, wu_ref, wv_ref, wg_ref, lng_ref, lnb_ref, ws_ref, bst_ref, o_ref):
    xb = x_ref[...].astype(BF16)
    u = _dot(xb, wu_ref[...])
    v = _layer_norm(_dot(xb, wv_ref[...]), lng_ref[...], lnb_ref[...])
    gate = _dot(xb, wg_ref[...])
    ug = u * _silu(gate)
    row = lax.broadcasted_iota(jnp.int32, (CHUNK, CHUNK), 0)
    col = lax.broadcasted_iota(jnp.int32, (CHUNK, CHUNK), 1)
    tril = row >= col
    bst = bst_ref[...]
    for g in range(A_GROUPS):
        w = jnp.where(tril, ws_ref[g], 0.0).astype(BF16)
        cols = slice(g * A_GROUP_DIM, (g + 1) * A_GROUP_DIM)
        for c in range(TM_PROJ // CHUNK):
            rows = slice(c * CHUNK, (c + 1) * CHUNK)
            mixed = _dot(w, v[rows, cols].astype(BF16)) + bst[:, g:g + 1]
            o_ref[rows, cols] = (ug[rows, cols] * mixed).astype(o_ref.dtype)


def _proj_a(x2d, w_in, a_ln_g, a_ln_b, a_w_s, a_b_s_t):
    m = x2d.shape[0]
    wspec = lambda j: _resident((D_MODEL, MIX_A), (0, j))
    return pl.pallas_call(
        _proj_a_kernel,
        out_shape=jax.ShapeDtypeStruct((m, MIX_A), BF16),
        grid=(m // TM_PROJ,),
        in_specs=[
            pl.BlockSpec((TM_PROJ, D_MODEL), lambda i: (i, 0)),
            wspec(0), wspec(1), wspec(2),
            _resident((1, MIX_A), (0, 0)), _resident((1, MIX_A), (0, 0)),
            _resident((A_GROUPS, CHUNK, CHUNK), (0, 0, 0)),
            _resident((CHUNK, A_GROUPS), (0, 0)),
        ],
        out_specs=pl.BlockSpec((TM_PROJ, MIX_A), lambda i: (i, 0)),
        compiler_params=_params(("parallel",)),
        name="proj_a_gmlp",
    )(x2d, w_in, w_in, w_in, a_ln_g, a_ln_b, a_w_s, a_b_s_t)


def _proj_b_kernel(x_ref, wq_ref, wk_ref, wv_ref, wg_ref, q_ref, k_ref, v_ref, kb_ref, vb_ref, sg_ref):
    xb = x_ref[...].astype(BF16)
    q_ref[...] = (_dot(xb, wq_ref[...]) * ATTN_SCALE).astype(q_ref.dtype)
    k = _dot(xb, wk_ref[...])
    k_ref[...] = k
    kb_ref[...] = k.astype(kb_ref.dtype)
    v = _dot(xb, wv_ref[...])
    v_ref[...] = v
    vb_ref[...] = v.astype(vb_ref.dtype)
    sg_ref[...] = _silu(_dot(xb, wg_ref[...])).astype(sg_ref.dtype)


def _proj_b(x2d, w_in):
    m = x2d.shape[0]
    first = 3 * MIX_A // QK_W
    wspec = lambda j: _resident((D_MODEL, QK_W), (0, first + j))
    row = lambda dt: jax.ShapeDtypeStruct((m, QK_W), dt)
    ospec = pl.BlockSpec((TM_PROJ, QK_W), lambda i: (i, 0))
    return pl.pallas_call(
        _proj_b_kernel,
        out_shape=(row(BF16), row(F32), row(F32), row(BF16), row(BF16), row(BF16)),
        grid=(m // TM_PROJ,),
        in_specs=[pl.BlockSpec((TM_PROJ, D_MODEL), lambda i: (i, 0)),
                  wspec(0), wspec(1), wspec(2), wspec(3)],
        out_specs=(ospec,) * 6,
        compiler_params=_params(("parallel",)),
        name="proj_b_qkv",
    )(x2d, w_in, w_in, w_in, w_in)


def _attn_kernel(q_ref, k_ref, v_ref, sg_ref, lq1_ref, lk1_ref, lq2_ref, lk2_ref, g_ref, o_ref):
    qi = pl.program_id(2)
    q = q_ref[...]
    lane = lax.broadcasted_iota(jnp.int32, q.shape, 1)
    zero = jnp.zeros_like(q)
    qq = jnp.concatenate([jnp.where(lane < B_QK_DIM, q, zero),
                          jnp.where(lane >= B_QK_DIM, q, zero)], axis=0)

    def step(c, carry, masked):
        m, l, acc = carry
        start = pl.multiple_of(c * TQ, TQ)
        kc = k_ref[pl.ds(start, TQ), :]
        vc = v_ref[pl.ds(start, TQ), :]
        s = _dot_nt(qq, kc)
        if masked:
            r = lax.broadcasted_iota(jnp.int32, s.shape, 0)
            r = jnp.where(r >= TQ, r - TQ, r)
            cidx = lax.broadcasted_iota(jnp.int32, s.shape, 1)
            s = jnp.where(cidx <= r, s, NEG)
        m_new = jnp.maximum(m, jnp.max(s, axis=-1, keepdims=True))
        alpha = jnp.exp(m - m_new)
        p = jnp.exp(s - m_new)
        l = alpha * l + jnp.sum(p, axis=-1, keepdims=True)
        acc = alpha * acc + _dot(p.astype(BF16), vc)
        return m_new, l, acc

    init = (jnp.full((2 * TQ, 1), NEG, F32), jnp.zeros((2 * TQ, 1), F32),
            jnp.zeros((2 * TQ, B_V_DIM), F32))
    carry = lax.fori_loop(0, qi, lambda c, cr: step(c, cr, False), init)
    m, l, acc = step(qi, carry, True)
    lam = _lam(lq1_ref[...], lk1_ref[...], lq2_ref[...], lk2_ref[...])
    o = acc[:TQ] / l[:TQ] - lam * (acc[TQ:] / l[TQ:])
    o_ref[...] = (_sub_ln(o, g_ref[...]) * sg_ref[...].astype(F32)).astype(o_ref.dtype)


def _attention(q, kb, vb, sg, lq1, lk1, lq2, lk2, subln_g, batch, seq):
    nq = seq // TQ
    qspec = pl.BlockSpec((TQ, B_K_ROW), lambda b, h, i: (b * nq + i, h))
    kvspec = pl.BlockSpec((seq, B_K_ROW), lambda b, h, i: (b, h))
    small = lambda n: pl.BlockSpec((1, n), lambda b, h, i: (0, 0))
    return pl.pallas_call(
        _attn_kernel,
        out_shape=jax.ShapeDtypeStruct((batch * seq, MIX_B), BF16),
        grid=(batch, B_HEADS, nq),
        in_specs=[qspec, kvspec, kvspec, qspec,
                  small(B_QK_DIM), small(B_QK_DIM), small(B_QK_DIM), small(B_QK_DIM),
                  small(B_V_DIM)],
        out_specs=qspec,
        compiler_params=_params(("parallel", "parallel", "arbitrary")),
        name="prompt_diff_attn",
    )(q, kb, vb, sg, lq1, lk1, lq2, lk2, subln_g)


def _out_ab_kernel(a_ref, b_ref, x_ref, wa_ref, wb_ref, g_ref, beta_ref, o_ref):
    out = _dot(a_ref[...].astype(BF16), wa_ref[...]) + _dot(b_ref[...].astype(BF16), wb_ref[...])
    o_ref[...] = _layer_norm(ALPHA * x_ref[...] + out, g_ref[...], beta_ref[...])


def _out_ab(a_out, b_out, x2d, w_out, ln_g, ln_b, tm):
    m = x2d.shape[0]
    half = pl.BlockSpec((tm, MIX_A), lambda i: (i, 0))
    full = pl.BlockSpec((tm, D_MODEL), lambda i: (i, 0))
    return pl.pallas_call(
        _out_ab_kernel,
        out_shape=jax.ShapeDtypeStruct((m, D_MODEL), F32),
        grid=(m // tm,),
        in_specs=[half, half, full,
                  _resident((MIX_A, D_MODEL), (0, 0)), _resident((MIX_B, D_MODEL), (1, 0)),
                  _resident((1, D_MODEL), (0, 0)), _resident((1, D_MODEL), (0, 0))],
        out_specs=full,
        compiler_params=_params(("parallel",)),
        name="out_ab_ln",
    )(a_out, b_out, x2d, w_out, w_out, ln_g, ln_b)


def _group_mix(pooled_g, g, wgrp_ref, bgrp_ref):
    cols = slice(g * C_GROUP_DIM, (g + 1) * C_GROUP_DIM)
    return _dot(pooled_g.astype(BF16), wgrp_ref[g]) + bgrp_ref[:, cols]


def _pool_layer_kernel(tiles_per_seq, y_ref, whp_ref, wgate_ref, wgrp_ref, bgrp_ref, scale_ref,
                       wout_ref, g_ref, beta_ref, o_ref, pool_ref, ext_ref):
    tm = y_ref.shape[0]
    t = pl.program_id(0) % tiles_per_seq
    y = y_ref[...]
    yb = y.astype(BF16)
    hp = _dot(yb, whp_ref[...])
    gate = _dot(yb, wgate_ref[...])

    @pl.when(t == 0)
    def _():
        ext_ref[0:HIST_PAD, :] = jnp.zeros((HIST_PAD, C_WIDTH), F32)

    @pl.when(t != 0)
    def _():
        ext_ref[0:HIST_PAD, :] = ext_ref[tm:tm + HIST_PAD, :]

    ext_ref[HIST_PAD:, :] = hp
    pos = t * tm + lax.broadcasted_iota(jnp.int32, (tm, 1), 0)
    mixed = []
    for g, w in enumerate(C_WINDOWS):
        cols = slice(g * C_GROUP_DIM, (g + 1) * C_GROUP_DIM)
        s = ext_ref[:, cols]
        shift = 1
        while shift < w:
            s = s + pltpu.roll(s, shift, axis=0)
            shift *= 2
        inv_count = 1.0 / jnp.minimum(pos + 1, w).astype(F32)
        pooled = s[HIST_PAD:] * inv_count - hp[:, cols]
        mixed.append(_group_mix(pooled, g, wgrp_ref, bgrp_ref))
    z = jnp.concatenate(mixed, axis=-1) * scale_ref[...] * _silu(gate)
    out = _dot(z.astype(BF16), wout_ref[...])
    o_ref[...] = _layer_norm(ALPHA * y + out, g_ref[...], beta_ref[...])
    pool_ref[...] = ext_ref[tm + HIST_PAD - C_HIST:tm + HIST_PAD, :]


def _pool_layer(y2d, w_in_c, w_grp, b_grp, scale, w_out_c, ln_g, ln_b, batch, seq):
    m = y2d.shape[0]
    tm = TM_PROJ
    tiles_per_seq = seq // tm
    full = pl.BlockSpec((tm, D_MODEL), lambda i: (i, 0))
    return pl.pallas_call(
        functools.partial(_pool_layer_kernel, tiles_per_seq),
        out_shape=(jax.ShapeDtypeStruct((m, D_MODEL), F32),
                   jax.ShapeDtypeStruct((batch, C_HIST, C_WIDTH), F32)),
        grid=(m // tm,),
        in_specs=[full,
                  _resident((D_MODEL, C_WIDTH), (0, 0)), _resident((D_MODEL, C_WIDTH), (0, 1)),
                  _resident((C_GROUPS, C_GROUP_DIM, C_GROUP_DIM), (0, 0, 0)),
                  _resident((1, C_WIDTH), (0, 0)), _resident((1, C_WIDTH), (0, 0)),
                  _resident((C_WIDTH, D_MODEL), (0, 0)),
                  _resident((1, D_MODEL), (0, 0)), _resident((1, D_MODEL), (0, 0))],
        out_specs=(full,
                   pl.BlockSpec((None, C_HIST, C_WIDTH), lambda i: (i // tiles_per_seq, 0, 0))),
        scratch_shapes=[pltpu.VMEM((tm + HIST_PAD, C_WIDTH), F32)],
        compiler_params=_params(("arbitrary",)),
        name="pool_layer",
    )(y2d, w_in_c, w_in_c, w_grp, b_grp, scale, w_out_c, ln_g, ln_b)


def _rows_proj_kernel(x_ref, w_ref, o_ref):
    o_ref[...] = _dot(x_ref[...].astype(BF16), w_ref[...])


def _rows_proj(x, w, tn):
    rows, k = x.shape
    n = w.shape[1]
    return pl.pallas_call(
        _rows_proj_kernel,
        out_shape=jax.ShapeDtypeStruct((rows, n), F32),
        grid=(n // tn,),
        in_specs=[_resident((rows, k), (0, 0)), pl.BlockSpec((k, tn), lambda j: (0, j))],
        out_specs=pl.BlockSpec((rows, tn), lambda j: (0, j)),
        compiler_params=_params(("parallel",)),
        name="rows_proj",
    )(x, w)


def _decode_attn_kernel(n_pages, pt_ref, q_ref, kn_ref, vn_ref, bg_ref,
                        lq1_ref, lk1_ref, lq2_ref, lk2_ref, g_ref, *rest):
    k_refs = rest[:n_pages]
    v_refs = rest[n_pages:2 * n_pages]
    o_ref = rest[2 * n_pages]
    s_ref = rest[2 * n_pages + 1]
    page_rows = PAGE_SIZE * B_HEADS
    q = q_ref[...] * ATTN_SCALE
    lane = lax.broadcasted_iota(jnp.int32, q.shape, 1)
    qq = jnp.concatenate([jnp.where(lane < B_QK_DIM, q, 0.0),
                          jnp.where(lane >= B_QK_DIM, q, 0.0)], axis=0)
    qqb = qq.astype(BF16)
    rh = lax.broadcasted_iota(jnp.int32, (2 * B_HEADS, page_rows), 0) % B_HEADS
    ch = lax.broadcasted_iota(jnp.int32, (2 * B_HEADS, page_rows), 1) % B_HEADS
    own = rh == ch
    kn2 = jnp.concatenate([kn_ref[...], kn_ref[...]], axis=0)
    vn2 = jnp.concatenate([vn_ref[...], vn_ref[...]], axis=0)
    s_new = jnp.sum(qq * kn2, axis=-1, keepdims=True)
    m = s_new
    for p in range(n_pages):
        kp = k_refs[p][...].reshape(page_rows, B_K_ROW).astype(BF16)
        s = jnp.where(own, _dot_nt(qqb, kp), NEG)
        s_ref[:, p * page_rows:(p + 1) * page_rows] = s
        m = jnp.maximum(m, jnp.max(s, axis=-1, keepdims=True))
    e_new = jnp.exp(s_new - m)
    l = e_new
    acc = e_new * vn2
    for p in range(n_pages):
        e = jnp.exp(s_ref[:, p * page_rows:(p + 1) * page_rows] - m)
        l = l + jnp.sum(e, axis=-1, keepdims=True)
        vp = v_refs[p][...].reshape(page_rows, B_V_DIM).astype(BF16)
        acc = acc + _dot(e.astype(BF16), vp)
    lam = _lam(lq1_ref[...], lk1_ref[...], lq2_ref[...], lk2_ref[...])
    o = acc[:B_HEADS] / l[:B_HEADS] - lam * (acc[B_HEADS:] / l[B_HEADS:])
    o_ref[...] = _sub_ln(o, g_ref[...]) * _silu(bg_ref[...])


def _decode_attention(page_table, q3, kn3, vn3, bg3, lq1, lk1, lq2, lk2, subln_g, cache_k, cache_v):
    dec_b, n_pages = page_table.shape
    row = pl.BlockSpec((None, B_HEADS, B_K_ROW), lambda b, pt: (b, 0, 0))
    small = lambda n: pl.BlockSpec((1, n), lambda b, pt: (0, 0))

    def page(p):
        return pl.BlockSpec((None, None, PAGE_SIZE, B_HEADS, B_K_ROW),
                            lambda b, pt: (0, pt[b, p], 0, 0, 0))

    pages = [page(p) for p in range(n_pages)]
    grid_spec = pltpu.PrefetchScalarGridSpec(
        num_scalar_prefetch=1,
        grid=(dec_b,),
        in_specs=[row, row, row, row,
                  small(B_QK_DIM), small(B_QK_DIM), small(B_QK_DIM), small(B_QK_DIM),
                  small(B_V_DIM)] + pages + pages,
        out_specs=row,
        scratch_shapes=[pltpu.VMEM((2 * B_HEADS, n_pages * PAGE_SIZE * B_HEADS), F32)],
    )
    return pl.pallas_call(
        functools.partial(_decode_attn_kernel, n_pages),
        out_shape=jax.ShapeDtypeStruct((dec_b, B_HEADS, B_V_DIM), F32),
        grid_spec=grid_spec,
        compiler_params=_params(("arbitrary",)),
        name="decode_diff_attn",
    )(page_table, q3, kn3, vn3, bg3, lq1, lk1, lq2, lk2, subln_g,
      *([cache_k] * n_pages), *([cache_v] * n_pages))


def _sample_out_ab_kernel(u_ref, v_ref, gate_ref, b_ref, x_ref, lng_ref, lnb_ref, ws_ref, bs_ref,
                          wa_ref, wb_ref, g_ref, beta_ref, y_ref, vn_ref):
    vn = _layer_norm(v_ref[...], lng_ref[...], lnb_ref[...])
    vn_ref[...] = vn
    ug = u_ref[...] * _silu(gate_ref[...])
    mixed = jnp.concatenate(
        [ws_ref[g][0:1, 0:1] * vn[:, g * A_GROUP_DIM:(g + 1) * A_GROUP_DIM] + bs_ref[g:g + 1, 0:1]
         for g in range(A_GROUPS)], axis=-1)
    a_out = ug * mixed
    out = _dot(a_out.astype(BF16), wa_ref[...]) + _dot(b_ref[...].astype(BF16), wb_ref[...])
    y_ref[...] = _layer_norm(ALPHA * x_ref[...] + out, g_ref[...], beta_ref[...])


def _sample_out_ab(h_s, b_out, x_s, a_ln_g, a_ln_b, a_w_s, a_b_s, w_out, ln_g, ln_b):
    rows = x_s.shape[0]
    hblock = lambda j: pl.BlockSpec((rows, MIX_A), lambda i: (0, j))
    whole = lambda shape: pl.BlockSpec(shape, lambda i: (0,) * len(shape))
    return pl.pallas_call(
        _sample_out_ab_kernel,
        out_shape=(jax.ShapeDtypeStruct((rows, D_MODEL), F32),
                   jax.ShapeDtypeStruct((rows, MIX_A), F32)),
        grid=(1,),
        in_specs=[hblock(0), hblock(1), hblock(2), whole((rows, MIX_B)), whole((rows, D_MODEL)),
                  whole((1, MIX_A)), whole((1, MIX_A)),
                  whole((A_GROUPS, CHUNK, CHUNK)), whole((A_GROUPS, CHUNK)),
                  pl.BlockSpec((MIX_A, D_MODEL), lambda i: (0, 0)),
                  pl.BlockSpec((MIX_B, D_MODEL), lambda i: (1, 0)),
                  whole((1, D_MODEL)), whole((1, D_MODEL))],
        out_specs=(whole((rows, D_MODEL)), whole((rows, MIX_A))),
        compiler_params=_params(("arbitrary",)),
        name="sample_out_ab_ln",
    )(h_s, h_s, h_s, b_out, x_s, a_ln_g, a_ln_b, a_w_s, a_b_s, w_out, w_out, ln_g, ln_b)


def _sample_pool_kernel(count_pos, hist_ref, hp_ref, pooled_ref, new_ref):
    hp = hp_ref[...]
    parts = []
    for g, w in enumerate(C_WINDOWS):
        cols = slice(g * C_GROUP_DIM, (g + 1) * C_GROUP_DIM)
        s = hp[:, :, cols]
        for k in range(1, w):
            s = s + hist_ref[:, C_HIST - k:C_HIST - k + 1, cols]
        parts.append(s * (1.0 / min(count_pos + 1, w)) - hp[:, :, cols])
    pooled_ref[...] = jnp.concatenate(parts, axis=-1)
    new_ref[:, 0:C_HIST - 1, :] = hist_ref[:, 1:C_HIST, :]
    new_ref[:, C_HIST - 1:C_HIST, :] = hp


def _sample_pool(state, hp3, past_len):
    dec_b = state.shape[0]
    hist = pl.BlockSpec((POOL_BB, C_HIST, C_WIDTH), lambda i: (i, 0, 0))
    one = pl.BlockSpec((POOL_BB, 1, C_WIDTH), lambda i: (i, 0, 0))
    return pl.pallas_call(
        functools.partial(_sample_pool_kernel, past_len),
        out_shape=(jax.ShapeDtypeStruct((dec_b, 1, C_WIDTH), F32),
                   jax.ShapeDtypeStruct((dec_b, C_HIST, C_WIDTH), F32)),
        grid=(dec_b // POOL_BB,),
        in_specs=[hist, one],
        out_specs=(one, hist),
        compiler_params=_params(("parallel",)),
        name="sample_pool",
    )(state, hp3)


def _sample_out_c_kernel(pooled_ref, gate_ref, y_ref, wgrp_ref, bgrp_ref, scale_ref, wout_ref,
                         g_ref, beta_ref, o_ref):
    pooled = pooled_ref[...]
    mixed = jnp.concatenate(
        [_group_mix(pooled[:, g * C_GROUP_DIM:(g + 1) * C_GROUP_DIM], g, wgrp_ref, bgrp_ref)
         for g in range(C_GROUPS)], axis=-1)
    z = mixed * scale_ref[...] * _silu(gate_ref[...])
    out = _dot(z.astype(BF16), wout_ref[...])
    o_ref[...] = _layer_norm(ALPHA * y_ref[...] + out, g_ref[...], beta_ref[...])


def _sample_out_c(pooled, h1, y1, w_grp, b_grp, scale, w_out_c, ln_g, ln_b):
    rows = y1.shape[0]
    whole = lambda shape: pl.BlockSpec(shape, lambda i: (0,) * len(shape))
    return pl.pallas_call(
        _sample_out_c_kernel,
        out_shape=jax.ShapeDtypeStruct((rows, D_MODEL), F32),
        grid=(1,),
        in_specs=[whole((rows, C_WIDTH)), pl.BlockSpec((rows, C_WIDTH), lambda i: (0, 1)),
                  whole((rows, D_MODEL)),
                  whole((C_GROUPS, C_GROUP_DIM, C_GROUP_DIM)),
                  whole((1, C_WIDTH)), whole((1, C_WIDTH)), whole((C_WIDTH, D_MODEL)),
                  whole((1, D_MODEL)), whole((1, D_MODEL))],
        out_specs=whole((rows, D_MODEL)),
        compiler_params=_params(("arbitrary",)),
        name="sample_out_c_ln",
    )(pooled, h1, y1, w_grp, b_grp, scale, w_out_c, ln_g, ln_b)


def kernel(x_prompt, x_sample, cache_k, cache_v, state_pool, page_table, ln_g, ln_b, w_in_ab, a_ln_g, a_ln_b, a_w_s, a_b_s, b_lq1, b_lk1, b_lq2, b_lk2, b_subln_g, w_out_ab, w_in_c, c_w_grp, c_b_grp, c_scale, w_out_c):
    batch, seq, _ = x_prompt.shape
    dec_b = x_sample.shape[0]
    n_pages = page_table.shape[1]
    past_len = n_pages * cache_k.shape[2]
    assert DEPTH == 2 and x_sample.shape[1] == 1 and past_len % CHUNK == 0
    assert seq % TQ == 0 and seq % TM_PROJ == 0 and TM_PROJ % CHUNK == 0 and dec_b % POOL_BB == 0

    w_in = w_in_ab[0].astype(BF16)
    w_out = w_out_ab[0].astype(BF16)
    w_in1 = w_in_c[0].astype(BF16)
    w_grp = c_w_grp[0].astype(BF16)
    w_out1 = w_out_c[0].astype(BF16)
    row = lambda v: v.reshape(1, -1)
    lq1, lk1, lq2, lk2 = row(b_lq1[0]), row(b_lk1[0]), row(b_lq2[0]), row(b_lk2[0])
    subln_g = row(b_subln_g[0])
    aln_g, aln_b = row(a_ln_g[0]), row(a_ln_b[0])
    ln_g0, ln_b0, ln_g1, ln_b1 = row(ln_g[0]), row(ln_b[0]), row(ln_g[1]), row(ln_b[1])
    b_grp, scale = row(c_b_grp[0]), row(c_scale[0])

    xp = x_prompt.reshape(batch * seq, D_MODEL)
    a_out = _proj_a(xp, w_in, aln_g, aln_b, a_w_s[0], a_b_s[0].T)
    q, k, v, kb, vb, sg = _proj_b(xp, w_in)
    b_out = _attention(q, kb, vb, sg, lq1, lk1, lq2, lk2, subln_g, batch, seq)
    y1 = _out_ab(a_out, b_out, xp, w_out, ln_g0, ln_b0, TM_PROJ)
    y2, pool_p = _pool_layer(y1, w_in1, w_grp, b_grp, scale, w_out1, ln_g1, ln_b1, batch, seq)

    xs = x_sample.reshape(dec_b, D_MODEL)
    h_s = _rows_proj(xs, w_in, MIX_A)
    seg = lambda j: h_s[:, j * MIX_A:(j + 1) * MIX_A].reshape(dec_b, B_HEADS, B_K_ROW)
    q3, kn3, vn3, bg3 = seg(3), seg(4), seg(5), seg(6)
    bo3 = _decode_attention(page_table, q3, kn3, vn3, bg3, lq1, lk1, lq2, lk2, subln_g,
                            cache_k, cache_v)
    y1s, cv_s = _sample_out_ab(h_s, bo3.reshape(dec_b, MIX_B), xs, aln_g, aln_b, a_w_s[0], a_b_s[0],
                               w_out, ln_g0, ln_b0)
    h1s = _rows_proj(y1s, w_in1, C_WIDTH // 2)
    pooled3, pool_s = _sample_pool(state_pool[0], h1s[:, :C_WIDTH].reshape(dec_b, 1, C_WIDTH),
                                   past_len)
    y2s = _sample_out_c(pooled3.reshape(dec_b, C_WIDTH), h1s, y1s, w_grp, b_grp, scale, w_out1,
                        ln_g1, ln_b1)

    return (y2.reshape(batch, seq, D_MODEL),
            y2s.reshape(dec_b, 1, D_MODEL),
            k.reshape(1, batch, seq, B_HEADS, B_K_ROW),
            v.reshape(1, batch, seq, B_HEADS, B_V_DIM),
            kn3.reshape(1, dec_b, 1, B_HEADS, B_K_ROW),
            vn3.reshape(1, dec_b, 1, B_HEADS, B_V_DIM),
            cv_s.reshape(1, dec_b, 1, MIX_A),
            pool_p.reshape(1, batch, C_HIST, C_WIDTH),
            pool_s.reshape(1, dec_b, C_HIST, C_WIDTH))
```

```python
import functools
import math

import jax
import jax.numpy as jnp
from jax import lax
from jax.experimental import pallas as pl
from jax.experimental.pallas import tpu as pltpu

D_MODEL = 2048
DEPTH = 2
PAGE_SIZE = 128
MIX_A = D_MODEL // 2
A_GROUPS = 4
A_GROUP_DIM = MIX_A // A_GROUPS
CHUNK = 128
MIX_B = D_MODEL // 2
B_HEADS = 8
B_QK_DIM = 64
B_K_ROW = 2 * B_QK_DIM
B_V_DIM = MIX_B // B_HEADS
QK_W = B_HEADS * 2 * B_QK_DIM
ATTN_SCALE = 1.0 / math.sqrt(B_QK_DIM)
AB_IN = 3 * MIX_A + 2 * QK_W + 2 * MIX_B
C_WIDTH = D_MODEL
C_WINDOWS = (2, 4, 8, 16)
C_GROUPS = len(C_WINDOWS)
C_GROUP_DIM = C_WIDTH // C_GROUPS
C_HIST = max(C_WINDOWS) - 1
ALPHA = (2 * DEPTH) ** 0.25
LN_EPS = 1e-5
NEG = -1e30
LOG2E = math.log2(math.e)
LAM_INIT_0 = 0.8 - 0.6 * math.exp(-0.3 * 0)

V7X_SUBLANES = 8
V7X_LANES = 128
V7X_VMEM_LIMIT_BYTES = 56 * 1024 * 1024

TM_PROJ = 256
TQ = 512
ATTN_RB = 64
HIST_PAD = 16
POOL_BB = 16

BF16 = jnp.bfloat16
F32 = jnp.float32


def _dot(a, b):
    return jnp.dot(a, b, preferred_element_type=F32)


def _dot_nt(a, b):
    return lax.dot_general(a, b, (((1,), (1,)), ((), ())), preferred_element_type=F32)


def _layer_norm(x, g, b):
    mu = jnp.mean(x, axis=-1, keepdims=True)
    xc = x - mu
    var = jnp.mean(xc * xc, axis=-1, keepdims=True)
    return xc * lax.rsqrt(var + LN_EPS) * g + b


def _silu(x):
    return x / (1.0 + jnp.exp(-x))


def _lam(lq1, lk1, lq2, lk2):
    s1 = jnp.sum(lq1 * lk1, axis=-1, keepdims=True)
    s2 = jnp.sum(lq2 * lk2, axis=-1, keepdims=True)
    return jnp.exp(s1) - jnp.exp(s2) + LAM_INIT_0


def _sub_ln(o, g):
    ms = jnp.mean(o * o, axis=-1, keepdims=True)
    return o * lax.rsqrt(ms + LN_EPS) * g * (1.0 - LAM_INIT_0)


def _params(semantics):
    return pltpu.CompilerParams(dimension_semantics=semantics,
                                vmem_limit_bytes=V7X_VMEM_LIMIT_BYTES)


def _resident(shape, index):
    return pl.BlockSpec(shape, lambda *_: index, pipeline_mode=pl.Buffered(1))


def _proj_a_kernel(x_ref, wu_ref, wv_ref, wg_ref, lng_ref, lnb_ref, ws_ref, bst_ref, o_ref):
    xb = x_ref[...].astype(BF16)
    u = _dot(xb, wu_ref[...])
    v = _layer_norm(_dot(xb, wv_ref[...]), lng_ref[...], lnb_ref[...])
    gate = _dot(xb, wg_ref[...])
    ug = u * _silu(gate)
    row = lax.broadcasted_iota(jnp.int32, (CHUNK, CHUNK), 0)
    col = lax.broadcasted_iota(jnp.int32, (CHUNK, CHUNK), 1)
    tril = row >= col
    bst = bst_ref[...]
    for g in range(A_GROUPS):
        w = jnp.where(tril, ws_ref[g], 0.0).astype(BF16)
        cols = slice(g * A_GROUP_DIM, (g + 1) * A_GROUP_DIM)
        for c in range(TM_PROJ // CHUNK):
            rows = slice(c * CHUNK, (c + 1) * CHUNK)
            mixed = _dot(w, v[rows, cols].astype(BF16)) + bst[:, g:g + 1]
            o_ref[rows, cols] = (ug[rows, cols] * mixed).astype(o_ref.dtype)


def _proj_a(x2d, w_in, a_ln_g, a_ln_b, a_w_s, a_b_s_t):
    m = x2d.shape[0]
    wspec = lambda j: _resident((D_MODEL, MIX_A), (0, j))
    return pl.pallas_call(
        _proj_a_kernel,
        out_shape=jax.ShapeDtypeStruct((m, MIX_A), BF16),
        grid=(m // TM_PROJ,),
        in_specs=[
            pl.BlockSpec((TM_PROJ, D_MODEL), lambda i: (i, 0)),
            wspec(0), wspec(1), wspec(2),
            _resident((1, MIX_A), (0, 0)), _resident((1, MIX_A), (0, 0)),
            _resident((A_GROUPS, CHUNK, CHUNK), (0, 0, 0)),
            _resident((CHUNK, A_GROUPS), (0, 0)),
        ],
        out_specs=pl.BlockSpec((TM_PROJ, MIX_A), lambda i: (i, 0)),
        compiler_params=_params(("parallel",)),
        name="proj_a_gmlp",
    )(x2d, w_in, w_in, w_in, a_ln_g, a_ln_b, a_w_s, a_b_s_t)


def _proj_b_kernel(x_ref, wq_ref, wk_ref, wv_ref, wg_ref, q_ref, k_ref, v_ref, kb_ref, vb_ref, sg_ref):
    xb = x_ref[...].astype(BF16)
    q_ref[...] = (_dot(xb, wq_ref[...]) * (ATTN_SCALE * LOG2E)).astype(q_ref.dtype)
    k = _dot(xb, wk_ref[...])
    k_ref[...] = k
    kb_ref[...] = k.astype(kb_ref.dtype)
    v = _dot(xb, wv_ref[...])
    v_ref[...] = v
    vb_ref[...] = v.astype(vb_ref.dtype)
    sg_ref[...] = _silu(_dot(xb, wg_ref[...])).astype(sg_ref.dtype)


def _proj_b(x2d, w_in):
    m = x2d.shape[0]
    first = 3 * MIX_A // QK_W
    wspec = lambda j: _resident((D_MODEL, QK_W), (0, first + j))
    row = lambda dt: jax.ShapeDtypeStruct((m, QK_W), dt)
    ospec = pl.BlockSpec((TM_PROJ, QK_W), lambda i: (i, 0))
    return pl.pallas_call(
        _proj_b_kernel,
        out_shape=(row(BF16), row(F32), row(F32), row(BF16), row(BF16), row(BF16)),
        grid=(m // TM_PROJ,),
        in_specs=[pl.BlockSpec((TM_PROJ, D_MODEL), lambda i: (i, 0)),
                  wspec(0), wspec(1), wspec(2), wspec(3)],
        out_specs=(ospec,) * 6,
        compiler_params=_params(("parallel",)),
        name="proj_b_qkv",
    )(x2d, w_in, w_in, w_in, w_in)


def _attn_kernel(q_ref, k_ref, v_ref, sg_ref, lq1_ref, lk1_ref, lq2_ref, lk2_ref, g_ref, o_ref,
                 qq_ref, s_ref, p_ref, m_ref, l_ref, alpha_ref, acc_ref):
    qi = pl.program_id(2)
    q = q_ref[...]
    lane = lax.broadcasted_iota(jnp.int32, q.shape, 1)
    zero = jnp.zeros_like(q)
    qq_ref[0:TQ, :] = jnp.where(lane < B_QK_DIM, q, zero)
    qq_ref[TQ:, :] = jnp.where(lane >= B_QK_DIM, q, zero)
    m_ref[...] = jnp.full(m_ref.shape, NEG, F32)
    l_ref[...] = jnp.zeros(l_ref.shape, F32)
    acc_ref[...] = jnp.zeros(acc_ref.shape, F32)

    def step(c, masked):
        start = pl.multiple_of(c * TQ, TQ)
        s_ref[...] = _dot_nt(qq_ref[...], k_ref[pl.ds(start, TQ), :])

        for r in range(2 * TQ // ATTN_RB):
            rows = slice(r * ATTN_RB, (r + 1) * ATTN_RB)
            tiles = []
            for j in range(TQ // V7X_LANES):
                t = s_ref[rows, j * V7X_LANES:(j + 1) * V7X_LANES]
                if masked:
                    qrow = (r * ATTN_RB) % TQ + lax.broadcasted_iota(jnp.int32, t.shape, 0)
                    kcol = j * V7X_LANES + lax.broadcasted_iota(jnp.int32, t.shape, 1)
                    t = jnp.where(kcol <= qrow, t, NEG)
                tiles.append(t)
            m_old = m_ref[rows, :]
            m_new = jnp.maximum(m_old, jnp.max(functools.reduce(jnp.maximum, tiles),
                                               axis=-1, keepdims=True))
            alpha = jnp.exp2(m_old - m_new)
            ps = [jnp.exp2(t - m_new) for t in tiles]
            l_ref[rows, :] = alpha * l_ref[rows, :] + functools.reduce(jnp.add, ps)
            m_ref[rows, :] = m_new
            alpha_ref[rows, :] = alpha
            for j, pj in enumerate(ps):
                p_ref[rows, j * V7X_LANES:(j + 1) * V7X_LANES] = pj.astype(BF16)
        acc_ref[...] = alpha_ref[...] * acc_ref[...] + _dot(p_ref[...], v_ref[pl.ds(start, TQ), :])

    def body(c, _):
        step(c, False)
        return 0

    lax.fori_loop(0, qi, body, 0)
    step(qi, True)
    lam = _lam(lq1_ref[...], lk1_ref[...], lq2_ref[...], lk2_ref[...])
    l = jnp.sum(l_ref[...], axis=-1, keepdims=True)
    o = acc_ref[0:TQ, :] / l[:TQ] - lam * (acc_ref[TQ:, :] / l[TQ:])
    o_ref[...] = (_sub_ln(o, g_ref[...]) * sg_ref[...].astype(F32)).astype(o_ref.dtype)


def _attention(q, kb, vb, sg, lq1, lk1, lq2, lk2, subln_g, batch, seq):
    nq = seq // TQ
    qspec = pl.BlockSpec((TQ, B_K_ROW), lambda b, h, i: (b * nq + i, h))
    kvspec = pl.BlockSpec((seq, B_K_ROW), lambda b, h, i: (b, h))
    small = lambda n: pl.BlockSpec((1, n), lambda b, h, i: (0, 0))
    return pl.pallas_call(
        _attn_kernel,
        out_shape=jax.ShapeDtypeStruct((batch * seq, MIX_B), BF16),
        grid=(batch, B_HEADS, nq),
        in_specs=[qspec, kvspec, kvspec, qspec,
                  small(B_QK_DIM), small(B_QK_DIM), small(B_QK_DIM), small(B_QK_DIM),
                  small(B_V_DIM)],
        out_specs=qspec,
        scratch_shapes=[pltpu.VMEM((2 * TQ, B_K_ROW), BF16),
                        pltpu.VMEM((2 * TQ, TQ), F32),
                        pltpu.VMEM((2 * TQ, TQ), BF16),
                        pltpu.VMEM((2 * TQ, V7X_LANES), F32),
                        pltpu.VMEM((2 * TQ, V7X_LANES), F32),
                        pltpu.VMEM((2 * TQ, V7X_LANES), F32),
                        pltpu.VMEM((2 * TQ, B_V_DIM), F32)],
        compiler_params=_params(("parallel", "parallel", "arbitrary")),
        name="prompt_diff_attn",
    )(q, kb, vb, sg, lq1, lk1, lq2, lk2, subln_g)


def _out_ab_kernel(a_ref, b_ref, x_ref, wa_ref, wb_ref, g_ref, beta_ref, o_ref):
    out = _dot(a_ref[...].astype(BF16), wa_ref[...]) + _dot(b_ref[...].astype(BF16), wb_ref[...])
    o_ref[...] = _layer_norm(ALPHA * x_ref[...] + out, g_ref[...], beta_ref[...])


def _out_ab(a_out, b_out, x2d, w_out, ln_g, ln_b, tm):
    m = x2d.shape[0]
    half = pl.BlockSpec((tm, MIX_A), lambda i: (i, 0))
    full = pl.BlockSpec((tm, D_MODEL), lambda i: (i, 0))
    return pl.pallas_call(
        _out_ab_kernel,
        out_shape=jax.ShapeDtypeStruct((m, D_MODEL), F32),
        grid=(m // tm,),
        in_specs=[half, half, full,
                  _resident((MIX_A, D_MODEL), (0, 0)), _resident((MIX_B, D_MODEL), (1, 0)),
                  _resident((1, D_MODEL), (0, 0)), _resident((1, D_MODEL), (0, 0))],
        out_specs=full,
        compiler_params=_params(("parallel",)),
        name="out_ab_ln",
    )(a_out, b_out, x2d, w_out, w_out, ln_g, ln_b)


def _group_mix(pooled_g, g, wgrp_ref, bgrp_ref):
    cols = slice(g * C_GROUP_DIM, (g + 1) * C_GROUP_DIM)
    return _dot(pooled_g.astype(BF16), wgrp_ref[g]) + bgrp_ref[:, cols]


def _pool_layer_kernel(tiles_per_seq, y_ref, whp_ref, wgate_ref, wgrp_ref, bgrp_ref, scale_ref,
                       wout_ref, g_ref, beta_ref, o_ref, pool_ref, ext_ref):
    tm = y_ref.shape[0]
    t = pl.program_id(0) % tiles_per_seq
    y = y_ref[...]
    yb = y.astype(BF16)
    hp = _dot(yb, whp_ref[...])
    gate = _dot(yb, wgate_ref[...])

    @pl.when(t == 0)
    def _():
        ext_ref[0:HIST_PAD, :] = jnp.zeros((HIST_PAD, C_WIDTH), F32)

    @pl.when(t != 0)
    def _():
        ext_ref[0:HIST_PAD, :] = ext_ref[tm:tm + HIST_PAD, :]

    ext_ref[HIST_PAD:, :] = hp
    pos = t * tm + lax.broadcasted_iota(jnp.int32, (tm, 1), 0)
    mixed = []
    for g, w in enumerate(C_WINDOWS):
        cols = slice(g * C_GROUP_DIM, (g + 1) * C_GROUP_DIM)
        s = ext_ref[:, cols]
        shift = 1
        while shift < w:
            s = s + pltpu.roll(s, shift, axis=0)
            shift *= 2
        inv_count = 1.0 / jnp.minimum(pos + 1, w).astype(F32)
        pooled = s[HIST_PAD:] * inv_count - hp[:, cols]
        mixed.append(_group_mix(pooled, g, wgrp_ref, bgrp_ref))
    z = jnp.concatenate(mixed, axis=-1) * scale_ref[...] * _silu(gate)
    out = _dot(z.astype(BF16), wout_ref[...])
    o_ref[...] = _layer_norm(ALPHA * y + out, g_ref[...], beta_ref[...])
    pool_ref[...] = ext_ref[tm + HIST_PAD - C_HIST:tm + HIST_PAD, :]


def _pool_layer(y2d, w_in_c, w_grp, b_grp, scale, w_out_c, ln_g, ln_b, batch, seq):
    m = y2d.shape[0]
    tm = TM_PROJ
    tiles_per_seq = seq // tm
    full = pl.BlockSpec((tm, D_MODEL), lambda i: (i, 0))
    return pl.pallas_call(
        functools.partial(_pool_layer_kernel, tiles_per_seq),
        out_shape=(jax.ShapeDtypeStruct((m, D_MODEL), F32),
                   jax.ShapeDtypeStruct((batch, C_HIST, C_WIDTH), F32)),
        grid=(m // tm,),
        in_specs=[full,
                  _resident((D_MODEL, C_WIDTH), (0, 0)), _resident((D_MODEL, C_WIDTH), (0, 1)),
                  _resident((C_GROUPS, C_GROUP_DIM, C_GROUP_DIM), (0, 0, 0)),
                  _resident((1, C_WIDTH), (0, 0)), _resident((1, C_WIDTH), (0, 0)),
                  _resident((C_WIDTH, D_MODEL), (0, 0)),
                  _resident((1, D_MODEL), (0, 0)), _resident((1, D_MODEL), (0, 0))],
        out_specs=(full,
                   pl.BlockSpec((None, C_HIST, C_WIDTH), lambda i: (i // tiles_per_seq, 0, 0))),
        scratch_shapes=[pltpu.VMEM((tm + HIST_PAD, C_WIDTH), F32)],
        compiler_params=_params(("arbitrary",)),
        name="pool_layer",
    )(y2d, w_in_c, w_in_c, w_grp, b_grp, scale, w_out_c, ln_g, ln_b)


def _rows_proj_kernel(x_ref, w_ref, o_ref):
    o_ref[...] = _dot(x_ref[...].astype(BF16), w_ref[...])


def _rows_proj(x, w, tn):
    rows, k = x.shape
    n = w.shape[1]
    return pl.pallas_call(
        _rows_proj_kernel,
        out_shape=jax.ShapeDtypeStruct((rows, n), F32),
        grid=(n // tn,),
        in_specs=[_resident((rows, k), (0, 0)), pl.BlockSpec((k, tn), lambda j: (0, j))],
        out_specs=pl.BlockSpec((rows, tn), lambda j: (0, j)),
        compiler_params=_params(("parallel",)),
        name="rows_proj",
    )(x, w)


def _decode_attn_kernel(n_pages, pt_ref, q_ref, kn_ref, vn_ref, bg_ref,
                        lq1_ref, lk1_ref, lq2_ref, lk2_ref, g_ref, *rest):
    k_refs = rest[:n_pages]
    v_refs = rest[n_pages:2 * n_pages]
    o_ref = rest[2 * n_pages]
    s_ref = rest[2 * n_pages + 1]
    page_rows = PAGE_SIZE * B_HEADS
    q = q_ref[...] * ATTN_SCALE
    lane = lax.broadcasted_iota(jnp.int32, q.shape, 1)
    qq = jnp.concatenate([jnp.where(lane < B_QK_DIM, q, 0.0),
                          jnp.where(lane >= B_QK_DIM, q, 0.0)], axis=0)
    qqb = qq.astype(BF16)
    rh = lax.broadcasted_iota(jnp.int32, (2 * B_HEADS, page_rows), 0) % B_HEADS
    ch = lax.broadcasted_iota(jnp.int32, (2 * B_HEADS, page_rows), 1) % B_HEADS
    own = rh == ch
    kn2 = jnp.concatenate([kn_ref[...], kn_ref[...]], axis=0)
    vn2 = jnp.concatenate([vn_ref[...], vn_ref[...]], axis=0)
    s_new = jnp.sum(qq * kn2, axis=-1, keepdims=True)
    m = s_new
    for p in range(n_pages):
        kp = k_refs[p][...].reshape(page_rows, B_K_ROW).astype(BF16)
        s = jnp.where(own, _dot_nt(qqb, kp), NEG)
        s_ref[:, p * page_rows:(p + 1) * page_rows] = s
        m = jnp.maximum(m, jnp.max(s, axis=-1, keepdims=True))
    e_new = jnp.exp(s_new - m)
    l = e_new
    acc = e_new * vn2
    for p in range(n_pages):
        e = jnp.exp(s_ref[:, p * page_rows:(p + 1) * page_rows] - m)
        l = l + jnp.sum(e, axis=-1, keepdims=True)
        vp = v_refs[p][...].reshape(page_rows, B_V_DIM).astype(BF16)
        acc = acc + _dot(e.astype(BF16), vp)
    lam = _lam(lq1_ref[...], lk1_ref[...], lq2_ref[...], lk2_ref[...])
    o = acc[:B_HEADS] / l[:B_HEADS] - lam * (acc[B_HEADS:] / l[B_HEADS:])
    o_ref[...] = _sub_ln(o, g_ref[...]) * _silu(bg_ref[...])


def _decode_attention(page_table, q3, kn3, vn3, bg3, lq1, lk1, lq2, lk2, subln_g, cache_k, cache_v):
    dec_b, n_pages = page_table.shape
    row = pl.BlockSpec((None, B_HEADS, B_K_ROW), lambda b, pt: (b, 0, 0))
    small = lambda n: pl.BlockSpec((1, n), lambda b, pt: (0, 0))

    def page(p):
        return pl.BlockSpec((None, None, PAGE_SIZE, B_HEADS, B_K_ROW),
                            lambda b, pt: (0, pt[b, p], 0, 0, 0))

    pages = [page(p) for p in range(n_pages)]
    grid_spec = pltpu.PrefetchScalarGridSpec(
        num_scalar_prefetch=1,
        grid=(dec_b,),
        in_specs=[row, row, row, row,
                  small(B_QK_DIM), small(B_QK_DIM), small(B_QK_DIM), small(B_QK_DIM),
                  small(B_V_DIM)] + pages + pages,
        out_specs=row,
        scratch_shapes=[pltpu.VMEM((2 * B_HEADS, n_pages * PAGE_SIZE * B_HEADS), F32)],
    )
    return pl.pallas_call(
        functools.partial(_decode_attn_kernel, n_pages),
        out_shape=jax.ShapeDtypeStruct((dec_b, B_HEADS, B_V_DIM), F32),
        grid_spec=grid_spec,
        compiler_params=_params(("arbitrary",)),
        name="decode_diff_attn",
    )(page_table, q3, kn3, vn3, bg3, lq1, lk1, lq2, lk2, subln_g,
      *([cache_k] * n_pages), *([cache_v] * n_pages))


def _sample_out_ab_kernel(u_ref, v_ref, gate_ref, b_ref, x_ref, lng_ref, lnb_ref, ws_ref, bs_ref,
                          wa_ref, wb_ref, g_ref, beta_ref, y_ref, vn_ref):
    vn = _layer_norm(v_ref[...], lng_ref[...], lnb_ref[...])
    vn_ref[...] = vn
    ug = u_ref[...] * _silu(gate_ref[...])
    mixed = jnp.concatenate(
        [ws_ref[g][0:1, 0:1] * vn[:, g * A_GROUP_DIM:(g + 1) * A_GROUP_DIM] + bs_ref[g:g + 1, 0:1]
         for g in range(A_GROUPS)], axis=-1)
    a_out = ug * mixed
    out = _dot(a_out.astype(BF16), wa_ref[...]) + _dot(b_ref[...].astype(BF16), wb_ref[...])
    y_ref[...] = _layer_norm(ALPHA * x_ref[...] + out, g_ref[...], beta_ref[...])


def _sample_out_ab(h_s, b_out, x_s, a_ln_g, a_ln_b, a_w_s, a_b_s, w_out, ln_g, ln_b):
    rows = x_s.shape[0]
    hblock = lambda j: pl.BlockSpec((rows, MIX_A), lambda i: (0, j))
    whole = lambda shape: pl.BlockSpec(shape, lambda i: (0,) * len(shape))
    return pl.pallas_call(
        _sample_out_ab_kernel,
        out_shape=(jax.ShapeDtypeStruct((rows, D_MODEL), F32),
                   jax.ShapeDtypeStruct((rows, MIX_A), F32)),
        grid=(1,),
        in_specs=[hblock(0), hblock(1), hblock(2), whole((rows, MIX_B)), whole((rows, D_MODEL)),
                  whole((1, MIX_A)), whole((1, MIX_A)),
                  whole((A_GROUPS, CHUNK, CHUNK)), whole((A_GROUPS, CHUNK)),
                  pl.BlockSpec((MIX_A, D_MODEL), lambda i: (0, 0)),
                  pl.BlockSpec((MIX_B, D_MODEL), lambda i: (1, 0)),
                  whole((1, D_MODEL)), whole((1, D_MODEL))],
        out_specs=(whole((rows, D_MODEL)), whole((rows, MIX_A))),
        compiler_params=_params(("arbitrary",)),
        name="sample_out_ab_ln",
    )(h_s, h_s, h_s, b_out, x_s, a_ln_g, a_ln_b, a_w_s, a_b_s, w_out, w_out, ln_g, ln_b)


def _sample_pool_kernel(count_pos, hist_ref, hp_ref, pooled_ref, new_ref):
    hp = hp_ref[...]
    parts = []
    for g, w in enumerate(C_WINDOWS):
        cols = slice(g * C_GROUP_DIM, (g + 1) * C_GROUP_DIM)
        s = hp[:, :, cols]
        for k in range(1, w):
            s = s + hist_ref[:, C_HIST - k:C_HIST - k + 1, cols]
        parts.append(s * (1.0 / min(count_pos + 1, w)) - hp[:, :, cols])
    pooled_ref[...] = jnp.concatenate(parts, axis=-1)
    new_ref[:, 0:C_HIST - 1, :] = hist_ref[:, 1:C_HIST, :]
    new_ref[:, C_HIST - 1:C_HIST, :] = hp


def _sample_pool(state, hp3, past_len):
    dec_b = state.shape[0]
    hist = pl.BlockSpec((POOL_BB, C_HIST, C_WIDTH), lambda i: (i, 0, 0))
    one = pl.BlockSpec((POOL_BB, 1, C_WIDTH), lambda i: (i, 0, 0))
    return pl.pallas_call(
        functools.partial(_sample_pool_kernel, past_len),
        out_shape=(jax.ShapeDtypeStruct((dec_b, 1, C_WIDTH), F32),
                   jax.ShapeDtypeStruct((dec_b, C_HIST, C_WIDTH), F32)),
        grid=(dec_b // POOL_BB,),
        in_specs=[hist, one],
        out_specs=(one, hist),
        compiler_params=_params(("parallel",)),
        name="sample_pool",
    )(state, hp3)


def _sample_out_c_kernel(pooled_ref, gate_ref, y_ref, wgrp_ref, bgrp_ref, scale_ref, wout_ref,
                         g_ref, beta_ref, o_ref):
    pooled = pooled_ref[...]
    mixed = jnp.concatenate(
        [_group_mix(pooled[:, g * C_GROUP_DIM:(g + 1) * C_GROUP_DIM], g, wgrp_ref, bgrp_ref)
         for g in range(C_GROUPS)], axis=-1)
    z = mixed * scale_ref[...] * _silu(gate_ref[...])
    out = _dot(z.astype(BF16), wout_ref[...])
    o_ref[...] = _layer_norm(ALPHA * y_ref[...] + out, g_ref[...], beta_ref[...])


def _sample_out_c(pooled, h1, y1, w_grp, b_grp, scale, w_out_c, ln_g, ln_b):
    rows = y1.shape[0]
    whole = lambda shape: pl.BlockSpec(shape, lambda i: (0,) * len(shape))
    return pl.pallas_call(
        _sample_out_c_kernel,
        out_shape=jax.ShapeDtypeStruct((rows, D_MODEL), F32),
        grid=(1,),
        in_specs=[whole((rows, C_WIDTH)), pl.BlockSpec((rows, C_WIDTH), lambda i: (0, 1)),
                  whole((rows, D_MODEL)),
                  whole((C_GROUPS, C_GROUP_DIM, C_GROUP_DIM)),
                  whole((1, C_WIDTH)), whole((1, C_WIDTH)), whole((C_WIDTH, D_MODEL)),
                  whole((1, D_MODEL)), whole((1, D_MODEL))],
        out_specs=whole((rows, D_MODEL)),
        compiler_params=_params(("arbitrary",)),
        name="sample_out_c_ln",
    )(pooled, h1, y1, w_grp, b_grp, scale, w_out_c, ln_g, ln_b)


def kernel(x_prompt, x_sample, cache_k, cache_v, state_pool, page_table, ln_g, ln_b, w_in_ab, a_ln_g, a_ln_b, a_w_s, a_b_s, b_lq1, b_lk1, b_lq2, b_lk2, b_subln_g, w_out_ab, w_in_c, c_w_grp, c_b_grp, c_scale, w_out_c):
    batch, seq, _ = x_prompt.shape
    dec_b = x_sample.shape[0]
    n_pages = page_table.shape[1]
    past_len = n_pages * cache_k.shape[2]
    assert DEPTH == 2 and x_sample.shape[1] == 1 and past_len % CHUNK == 0
    assert seq % TQ == 0 and seq % TM_PROJ == 0 and TM_PROJ % CHUNK == 0 and dec_b % POOL_BB == 0

    w_in = w_in_ab[0].astype(BF16)
    w_out = w_out_ab[0].astype(BF16)
    w_in1 = w_in_c[0].astype(BF16)
    w_grp = c_w_grp[0].astype(BF16)
    w_out1 = w_out_c[0].astype(BF16)
    row = lambda v: v.reshape(1, -1)
    lq1, lk1, lq2, lk2 = row(b_lq1[0]), row(b_lk1[0]), row(b_lq2[0]), row(b_lk2[0])
    subln_g = row(b_subln_g[0])
    aln_g, aln_b = row(a_ln_g[0]), row(a_ln_b[0])
    ln_g0, ln_b0, ln_g1, ln_b1 = row(ln_g[0]), row(ln_b[0]), row(ln_g[1]), row(ln_b[1])
    b_grp, scale = row(c_b_grp[0]), row(c_scale[0])

    xp = x_prompt.reshape(batch * seq, D_MODEL)
    a_out = _proj_a(xp, w_in, aln_g, aln_b, a_w_s[0], a_b_s[0].T)
    q, k, v, kb, vb, sg = _proj_b(xp, w_in)
    b_out = _attention(q, kb, vb, sg, lq1, lk1, lq2, lk2, subln_g, batch, seq)
    y1 = _out_ab(a_out, b_out, xp, w_out, ln_g0, ln_b0, TM_PROJ)
    y2, pool_p = _pool_layer(y1, w_in1, w_grp, b_grp, scale, w_out1, ln_g1, ln_b1, batch, seq)

    xs = x_sample.reshape(dec_b, D_MODEL)
    h_s = _rows_proj(xs, w_in, MIX_A)
    seg = lambda j: h_s[:, j * MIX_A:(j + 1) * MIX_A].reshape(dec_b, B_HEADS, B_K_ROW)
    q3, kn3, vn3, bg3 = seg(3), seg(4), seg(5), seg(6)
    bo3 = _decode_attention(page_table, q3, kn3, vn3, bg3, lq1, lk1, lq2, lk2, subln_g,
                            cache_k, cache_v)
    y1s, cv_s = _sample_out_ab(h_s, bo3.reshape(dec_b, MIX_B), xs, aln_g, aln_b, a_w_s[0], a_b_s[0],
                               w_out, ln_g0, ln_b0)
    h1s = _rows_proj(y1s, w_in1, C_WIDTH // 2)
    pooled3, pool_s = _sample_pool(state_pool[0], h1s[:, :C_WIDTH].reshape(dec_b, 1, C_WIDTH),
                                   past_len)
    y2s = _sample_out_c(pooled3.reshape(dec_b, C_WIDTH), h1s, y1s, w_grp, b_grp, scale, w_out1,
                        ln_g1, ln_b1)

    return (y2.reshape(batch, seq, D_MODEL),
            y2s.reshape(dec_b, 1, D_MODEL),
            k.reshape(1, batch, seq, B_HEADS, B_K_ROW),
            v.reshape(1, batch, seq, B_HEADS, B_V_DIM),
            kn3.reshape(1, dec_b, 1, B_HEADS, B_K_ROW),
            vn3.reshape(1, dec_b, 1, B_HEADS, B_V_DIM),
            cv_s.reshape(1, dec_b, 1, MIX_A),
            pool_p.reshape(1, batch, C_HIST, C_WIDTH),
            pool_s.reshape(1, dec_b, C_HIST, C_WIDTH))
```

```python
import functools
import math

import jax
import jax.numpy as jnp
from jax import lax
from jax.experimental import pallas as pl
from jax.experimental.pallas import tpu as pltpu

D_MODEL = 2048
DEPTH = 2
PAGE_SIZE = 128
MIX_A = D_MODEL // 2
A_GROUPS = 4
A_GROUP_DIM = MIX_A // A_GROUPS
CHUNK = 128
MIX_B = D_MODEL // 2
B_HEADS = 8
B_QK_DIM = 64
B_K_ROW = 2 * B_QK_DIM
B_V_DIM = MIX_B // B_HEADS
QK_W = B_HEADS * 2 * B_QK_DIM
ATTN_SCALE = 1.0 / math.sqrt(B_QK_DIM)
AB_IN = 3 * MIX_A + 2 * QK_W + 2 * MIX_B
C_WIDTH = D_MODEL
C_WINDOWS = (2, 4, 8, 16)
C_GROUPS = len(C_WINDOWS)
C_GROUP_DIM = C_WIDTH // C_GROUPS
C_HIST = max(C_WINDOWS) - 1
ALPHA = (2 * DEPTH) ** 0.25
LN_EPS = 1e-5
NEG = -1e30
LOG2E = math.log2(math.e)
LAM_INIT_0 = 0.8 - 0.6 * math.exp(-0.3 * 0)

V7X_SUBLANES = 8
V7X_LANES = 128
V7X_VMEM_LIMIT_BYTES = 56 * 1024 * 1024

TM_PROJ = 256
TQ = 512
ATTN_RB = 64
HIST_PAD = 16
POOL_BB = 16

BF16 = jnp.bfloat16
F32 = jnp.float32


def _dot(a, b):
    return jnp.dot(a, b, preferred_element_type=F32)


def _dot_nt(a, b):
    return lax.dot_general(a, b, (((1,), (1,)), ((), ())), preferred_element_type=F32)


def _layer_norm(x, g, b):
    mu = jnp.mean(x, axis=-1, keepdims=True)
    xc = x - mu
    var = jnp.mean(xc * xc, axis=-1, keepdims=True)
    return xc * lax.rsqrt(var + LN_EPS) * g + b


def _silu(x):
    return x / (1.0 + jnp.exp(-x))


def _lam(lq1, lk1, lq2, lk2):
    s1 = jnp.sum(lq1 * lk1, axis=-1, keepdims=True)
    s2 = jnp.sum(lq2 * lk2, axis=-1, keepdims=True)
    return jnp.exp(s1) - jnp.exp(s2) + LAM_INIT_0


def _sub_ln(o, g):
    ms = jnp.mean(o * o, axis=-1, keepdims=True)
    return o * lax.rsqrt(ms + LN_EPS) * g * (1.0 - LAM_INIT_0)


def _params(semantics):
    return pltpu.CompilerParams(dimension_semantics=semantics,
                                vmem_limit_bytes=V7X_VMEM_LIMIT_BYTES)


def _resident(shape, index):
    return pl.BlockSpec(shape, lambda *_: index, pipeline_mode=pl.Buffered(1))


def _proj_a_kernel(x_ref, wu_ref, wv_ref, wg_ref, lng_ref, lnb_ref, ws_ref, bst_ref, o_ref):
    xb = x_ref[...].astype(BF16)
    u = _dot(xb, wu_ref[...])
    v = _layer_norm(_dot(xb, wv_ref[...]), lng_ref[...], lnb_ref[...])
    gate = _dot(xb, wg_ref[...])
    ug = u * _silu(gate)
    row = lax.broadcasted_iota(jnp.int32, (CHUNK, CHUNK), 0)
    col = lax.broadcasted_iota(jnp.int32, (CHUNK, CHUNK), 1)
    tril = row >= col
    bst = bst_ref[...]
    for g in range(A_GROUPS):
        w = jnp.where(tril, ws_ref[g], 0.0).astype(BF16)
        cols = slice(g * A_GROUP_DIM, (g + 1) * A_GROUP_DIM)
        for c in range(TM_PROJ // CHUNK):
            rows = slice(c * CHUNK, (c + 1) * CHUNK)
            mixed = _dot(w, v[rows, cols].astype(BF16)) + bst[:, g:g + 1]
            o_ref[rows, cols] = (ug[rows, cols] * mixed).astype(o_ref.dtype)


def _proj_a(x2d, w_in, a_ln_g, a_ln_b, a_w_s, a_b_s_t):
    m = x2d.shape[0]
    wspec = lambda j: _resident((D_MODEL, MIX_A), (0, j))
    return pl.pallas_call(
        _proj_a_kernel,
        out_shape=jax.ShapeDtypeStruct((m, MIX_A), BF16),
        grid=(m // TM_PROJ,),
        in_specs=[
            pl.BlockSpec((TM_PROJ, D_MODEL), lambda i: (i, 0)),
            wspec(0), wspec(1), wspec(2),
            _resident((1, MIX_A), (0, 0)), _resident((1, MIX_A), (0, 0)),
            _resident((A_GROUPS, CHUNK, CHUNK), (0, 0, 0)),
            _resident((CHUNK, A_GROUPS), (0, 0)),
        ],
        out_specs=pl.BlockSpec((TM_PROJ, MIX_A), lambda i: (i, 0)),
        compiler_params=_params(("parallel",)),
        name="proj_a_gmlp",
    )(x2d, w_in, w_in, w_in, a_ln_g, a_ln_b, a_w_s, a_b_s_t)


def _proj_b_kernel(x_ref, wq_ref, wk_ref, wv_ref, wg_ref, q_ref, k_ref, v_ref, kb_ref, vb_ref, sg_ref):
    xb = x_ref[...].astype(BF16)
    q_ref[...] = (_dot(xb, wq_ref[...]) * (ATTN_SCALE * LOG2E)).astype(q_ref.dtype)
    k = _dot(xb, wk_ref[...])
    k_ref[...] = k
    kb_ref[...] = k.astype(kb_ref.dtype)
    v = _dot(xb, wv_ref[...])
    v_ref[...] = v
    vb_ref[...] = v.astype(vb_ref.dtype)
    sg_ref[...] = _silu(_dot(xb, wg_ref[...])).astype(sg_ref.dtype)


def _proj_b(x2d, w_in):
    m = x2d.shape[0]
    first = 3 * MIX_A // QK_W
    wspec = lambda j: _resident((D_MODEL, QK_W), (0, first + j))
    row = lambda dt: jax.ShapeDtypeStruct((m, QK_W), dt)
    ospec = pl.BlockSpec((TM_PROJ, QK_W), lambda i: (i, 0))
    return pl.pallas_call(
        _proj_b_kernel,
        out_shape=(row(BF16), row(F32), row(F32), row(BF16), row(BF16), row(BF16)),
        grid=(m // TM_PROJ,),
        in_specs=[pl.BlockSpec((TM_PROJ, D_MODEL), lambda i: (i, 0)),
                  wspec(0), wspec(1), wspec(2), wspec(3)],
        out_specs=(ospec,) * 6,
        compiler_params=_params(("parallel",)),
        name="proj_b_qkv",
    )(x2d, w_in, w_in, w_in, w_in)


def _prompt_attn_body(qi, lam, q_ref, k_ref, v_ref, sg_ref, g_ref, o_ref,
                      qq_ref, s_ref, p_ref, m_ref, l_ref, alpha_ref, acc_ref):
    q = q_ref[...]
    lane = lax.broadcasted_iota(jnp.int32, q.shape, 1)
    zero = jnp.zeros_like(q)
    qq_ref[0:TQ, :] = jnp.where(lane < B_QK_DIM, q, zero)
    qq_ref[TQ:, :] = jnp.where(lane >= B_QK_DIM, q, zero)
    m_ref[...] = jnp.full(m_ref.shape, NEG, F32)
    l_ref[...] = jnp.zeros(l_ref.shape, F32)
    acc_ref[...] = jnp.zeros(acc_ref.shape, F32)

    def step(c, masked):
        start = pl.multiple_of(c * TQ, TQ)
        s_ref[...] = _dot_nt(qq_ref[...], k_ref[pl.ds(start, TQ), :])

        for r in range(2 * TQ // ATTN_RB):
            rows = slice(r * ATTN_RB, (r + 1) * ATTN_RB)
            tiles = []
            for j in range(TQ // V7X_LANES):
                t = s_ref[rows, j * V7X_LANES:(j + 1) * V7X_LANES]
                if masked:
                    qrow = (r * ATTN_RB) % TQ + lax.broadcasted_iota(jnp.int32, t.shape, 0)
                    kcol = j * V7X_LANES + lax.broadcasted_iota(jnp.int32, t.shape, 1)
                    t = jnp.where(kcol <= qrow, t, NEG)
                tiles.append(t)
            m_old = m_ref[rows, :]
            m_new = jnp.maximum(m_old, jnp.max(functools.reduce(jnp.maximum, tiles),
                                               axis=-1, keepdims=True))
            alpha = jnp.exp2(m_old - m_new)
            ps = [jnp.exp2(t - m_new) for t in tiles]
            l_ref[rows, :] = alpha * l_ref[rows, :] + functools.reduce(jnp.add, ps)
            m_ref[rows, :] = m_new
            alpha_ref[rows, :] = alpha
            for j, pj in enumerate(ps):
                p_ref[rows, j * V7X_LANES:(j + 1) * V7X_LANES] = pj.astype(BF16)
        acc_ref[...] = alpha_ref[...] * acc_ref[...] + _dot(p_ref[...], v_ref[pl.ds(start, TQ), :])

    def body(c, _):
        step(c, False)
        return 0

    lax.fori_loop(0, qi, body, 0)
    step(qi, True)
    l = jnp.sum(l_ref[...], axis=-1, keepdims=True)
    o = acc_ref[0:TQ, :] / l[:TQ] - lam * (acc_ref[TQ:, :] / l[TQ:])
    o_ref[...] = (_sub_ln(o, g_ref[...]) * sg_ref[...].astype(F32)).astype(o_ref.dtype)


def _decode_attn_body(lam, q_ref, kn_ref, vn_ref, bg_ref, g_ref, k_refs, v_refs, o_ref, s_ref):
    n_pages = len(k_refs)
    page_rows = PAGE_SIZE * B_HEADS
    q = q_ref[...] * ATTN_SCALE
    lane = lax.broadcasted_iota(jnp.int32, q.shape, 1)
    qq = jnp.concatenate([jnp.where(lane < B_QK_DIM, q, 0.0),
                          jnp.where(lane >= B_QK_DIM, q, 0.0)], axis=0)
    qqb = qq.astype(BF16)
    rh = lax.broadcasted_iota(jnp.int32, (2 * B_HEADS, page_rows), 0) % B_HEADS
    ch = lax.broadcasted_iota(jnp.int32, (2 * B_HEADS, page_rows), 1) % B_HEADS
    own = rh == ch
    kn2 = jnp.concatenate([kn_ref[...], kn_ref[...]], axis=0)
    vn2 = jnp.concatenate([vn_ref[...], vn_ref[...]], axis=0)
    s_new = jnp.sum(qq * kn2, axis=-1, keepdims=True)
    m = s_new
    for p in range(n_pages):
        kp = k_refs[p][...].reshape(page_rows, B_K_ROW).astype(BF16)
        s = jnp.where(own, _dot_nt(qqb, kp), NEG)
        s_ref[:, p * page_rows:(p + 1) * page_rows] = s
        m = jnp.maximum(m, jnp.max(s, axis=-1, keepdims=True))
    e_new = jnp.exp(s_new - m)
    l = e_new
    acc = e_new * vn2
    for p in range(n_pages):
        e = jnp.exp(s_ref[:, p * page_rows:(p + 1) * page_rows] - m)
        l = l + jnp.sum(e, axis=-1, keepdims=True)
        vp = v_refs[p][...].reshape(page_rows, B_V_DIM).astype(BF16)
        acc = acc + _dot(e.astype(BF16), vp)
    o = acc[:B_HEADS] / l[:B_HEADS] - lam * (acc[B_HEADS:] / l[B_HEADS:])
    o_ref[...] = _sub_ln(o, g_ref[...]) * _silu(bg_ref[...])


def _diff_attn_kernel(n_pages, pt_ref, q_ref, k_ref, v_ref, sg_ref,
                      lq1_ref, lk1_ref, lq2_ref, lk2_ref, g_ref,
                      dq_ref, dkn_ref, dvn_ref, dbg_ref, *rest):
    k_refs = rest[:n_pages]
    v_refs = rest[n_pages:2 * n_pages]
    o_ref, do_ref = rest[2 * n_pages:2 * n_pages + 2]
    qq_ref, s_ref, p_ref, m_ref, l_ref, alpha_ref, acc_ref, ds_ref = rest[2 * n_pages + 2:]
    lam = _lam(lq1_ref[...], lk1_ref[...], lq2_ref[...], lk2_ref[...])
    _decode_attn_body(lam, dq_ref, dkn_ref, dvn_ref, dbg_ref, g_ref, k_refs, v_refs, do_ref, ds_ref)
    _prompt_attn_body(pl.program_id(2), lam, q_ref, k_ref, v_ref, sg_ref, g_ref, o_ref,
                      qq_ref, s_ref, p_ref, m_ref, l_ref, alpha_ref, acc_ref)


def _diff_attention(q, kb, vb, sg, lq1, lk1, lq2, lk2, subln_g, batch, seq,
                    page_table, q3, kn3, vn3, bg3, cache_k, cache_v):
    nq = seq // TQ
    dec_b, n_pages = page_table.shape
    assert dec_b == batch * B_HEADS * nq
    step = lambda b, h, i: (b * B_HEADS + h) * nq + i
    qspec = pl.BlockSpec((TQ, B_K_ROW), lambda b, h, i, pt: (b * nq + i, h))
    kvspec = pl.BlockSpec((seq, B_K_ROW), lambda b, h, i, pt: (b, h))
    small = lambda n: pl.BlockSpec((1, n), lambda b, h, i, pt: (0, 0))
    row = pl.BlockSpec((None, B_HEADS, B_K_ROW), lambda b, h, i, pt: (step(b, h, i), 0, 0))

    def page(p):
        return pl.BlockSpec((None, None, PAGE_SIZE, B_HEADS, B_K_ROW),
                            lambda b, h, i, pt: (0, pt[step(b, h, i), p], 0, 0, 0))

    pages = [page(p) for p in range(n_pages)]
    grid_spec = pltpu.PrefetchScalarGridSpec(
        num_scalar_prefetch=1,
        grid=(batch, B_HEADS, nq),
        in_specs=[qspec, kvspec, kvspec, qspec,
                  small(B_QK_DIM), small(B_QK_DIM), small(B_QK_DIM), small(B_QK_DIM),
                  small(B_V_DIM), row, row, row, row] + pages + pages,
        out_specs=(qspec, row),
        scratch_shapes=[pltpu.VMEM((2 * TQ, B_K_ROW), BF16),
                        pltpu.VMEM((2 * TQ, TQ), F32),
                        pltpu.VMEM((2 * TQ, TQ), BF16),
                        pltpu.VMEM((2 * TQ, V7X_LANES), F32),
                        pltpu.VMEM((2 * TQ, V7X_LANES), F32),
                        pltpu.VMEM((2 * TQ, V7X_LANES), F32),
                        pltpu.VMEM((2 * TQ, B_V_DIM), F32),
                        pltpu.VMEM((2 * B_HEADS, n_pages * PAGE_SIZE * B_HEADS), F32)],
    )
    return pl.pallas_call(
        functools.partial(_diff_attn_kernel, n_pages),
        out_shape=(jax.ShapeDtypeStruct((batch * seq, MIX_B), BF16),
                   jax.ShapeDtypeStruct((dec_b, B_HEADS, B_V_DIM), F32)),
        grid_spec=grid_spec,
        compiler_params=_params(("arbitrary", "arbitrary", "arbitrary")),
        name="diff_attn",
    )(page_table, q, kb, vb, sg, lq1, lk1, lq2, lk2, subln_g, q3, kn3, vn3, bg3,
      *([cache_k] * n_pages), *([cache_v] * n_pages))


def _out_ab_kernel(a_ref, b_ref, x_ref, wa_ref, wb_ref, g_ref, beta_ref, o_ref):
    out = _dot(a_ref[...].astype(BF16), wa_ref[...]) + _dot(b_ref[...].astype(BF16), wb_ref[...])
    o_ref[...] = _layer_norm(ALPHA * x_ref[...] + out, g_ref[...], beta_ref[...])


def _out_ab(a_out, b_out, x2d, w_out, ln_g, ln_b, tm):
    m = x2d.shape[0]
    half = pl.BlockSpec((tm, MIX_A), lambda i: (i, 0))
    full = pl.BlockSpec((tm, D_MODEL), lambda i: (i, 0))
    return pl.pallas_call(
        _out_ab_kernel,
        out_shape=jax.ShapeDtypeStruct((m, D_MODEL), F32),
        grid=(m // tm,),
        in_specs=[half, half, full,
                  _resident((MIX_A, D_MODEL), (0, 0)), _resident((MIX_B, D_MODEL), (1, 0)),
                  _resident((1, D_MODEL), (0, 0)), _resident((1, D_MODEL), (0, 0))],
        out_specs=full,
        compiler_params=_params(("parallel",)),
        name="out_ab_ln",
    )(a_out, b_out, x2d, w_out, w_out, ln_g, ln_b)


def _group_mix(pooled_g, g, wgrp_ref, bgrp_ref):
    cols = slice(g * C_GROUP_DIM, (g + 1) * C_GROUP_DIM)
    return _dot(pooled_g.astype(BF16), wgrp_ref[g]) + bgrp_ref[:, cols]


def _pool_layer_kernel(tiles_per_seq, y_ref, whp_ref, wgate_ref, wgrp_ref, bgrp_ref, scale_ref,
                       wout_ref, g_ref, beta_ref, o_ref, pool_ref, ext_ref):
    tm = y_ref.shape[0]
    t = pl.program_id(0) % tiles_per_seq
    y = y_ref[...]
    yb = y.astype(BF16)
    hp = _dot(yb, whp_ref[...])
    gate = _dot(yb, wgate_ref[...])

    @pl.when(t == 0)
    def _():
        ext_ref[0:HIST_PAD, :] = jnp.zeros((HIST_PAD, C_WIDTH), F32)

    @pl.when(t != 0)
    def _():
        ext_ref[0:HIST_PAD, :] = ext_ref[tm:tm + HIST_PAD, :]

    ext_ref[HIST_PAD:, :] = hp
    pos = t * tm + lax.broadcasted_iota(jnp.int32, (tm, 1), 0)
    mixed = []
    for g, w in enumerate(C_WINDOWS):
        cols = slice(g * C_GROUP_DIM, (g + 1) * C_GROUP_DIM)
        s = ext_ref[:, cols]
        shift = 1
        while shift < w:
            s = s + pltpu.roll(s, shift, axis=0)
            shift *= 2
        inv_count = 1.0 / jnp.minimum(pos + 1, w).astype(F32)
        pooled = s[HIST_PAD:] * inv_count - hp[:, cols]
        mixed.append(_group_mix(pooled, g, wgrp_ref, bgrp_ref))
    z = jnp.concatenate(mixed, axis=-1) * scale_ref[...] * _silu(gate)
    out = _dot(z.astype(BF16), wout_ref[...])
    o_ref[...] = _layer_norm(ALPHA * y + out, g_ref[...], beta_ref[...])
    pool_ref[...] = ext_ref[tm + HIST_PAD - C_HIST:tm + HIST_PAD, :]


def _pool_layer(y2d, w_in_c, w_grp, b_grp, scale, w_out_c, ln_g, ln_b, batch, seq):
    m = y2d.shape[0]
    tm = TM_PROJ
    tiles_per_seq = seq // tm
    full = pl.BlockSpec((tm, D_MODEL), lambda i: (i, 0))
    return pl.pallas_call(
        functools.partial(_pool_layer_kernel, tiles_per_seq),
        out_shape=(jax.ShapeDtypeStruct((m, D_MODEL), F32),
                   jax.ShapeDtypeStruct((batch, C_HIST, C_WIDTH), F32)),
        grid=(m // tm,),
        in_specs=[full,
                  _resident((D_MODEL, C_WIDTH), (0, 0)), _resident((D_MODEL, C_WIDTH), (0, 1)),
                  _resident((C_GROUPS, C_GROUP_DIM, C_GROUP_DIM), (0, 0, 0)),
                  _resident((1, C_WIDTH), (0, 0)), _resident((1, C_WIDTH), (0, 0)),
                  _resident((C_WIDTH, D_MODEL), (0, 0)),
                  _resident((1, D_MODEL), (0, 0)), _resident((1, D_MODEL), (0, 0))],
        out_specs=(full,
                   pl.BlockSpec((None, C_HIST, C_WIDTH), lambda i: (i // tiles_per_seq, 0, 0))),
        scratch_shapes=[pltpu.VMEM((tm + HIST_PAD, C_WIDTH), F32)],
        compiler_params=_params(("arbitrary",)),
        name="pool_layer",
    )(y2d, w_in_c, w_in_c, w_grp, b_grp, scale, w_out_c, ln_g, ln_b)


def _rows_proj_kernel(x_ref, w_ref, o_ref):
    o_ref[...] = _dot(x_ref[...].astype(BF16), w_ref[...])


def _rows_proj(x, w, tn):
    rows, k = x.shape
    n = w.shape[1]
    return pl.pallas_call(
        _rows_proj_kernel,
        out_shape=jax.ShapeDtypeStruct((rows, n), F32),
        grid=(n // tn,),
        in_specs=[_resident((rows, k), (0, 0)), pl.BlockSpec((k, tn), lambda j: (0, j))],
        out_specs=pl.BlockSpec((rows, tn), lambda j: (0, j)),
        compiler_params=_params(("parallel",)),
        name="rows_proj",
    )(x, w)


def _sample_out_ab_kernel(u_ref, v_ref, gate_ref, b_ref, x_ref, lng_ref, lnb_ref, ws_ref, bs_ref,
                          wa_ref, wb_ref, g_ref, beta_ref, y_ref, vn_ref):
    vn = _layer_norm(v_ref[...], lng_ref[...], lnb_ref[...])
    vn_ref[...] = vn
    ug = u_ref[...] * _silu(gate_ref[...])
    mixed = jnp.concatenate(
        [ws_ref[g][0:1, 0:1] * vn[:, g * A_GROUP_DIM:(g + 1) * A_GROUP_DIM] + bs_ref[g:g + 1, 0:1]
         for g in range(A_GROUPS)], axis=-1)
    a_out = ug * mixed
    out = _dot(a_out.astype(BF16), wa_ref[...]) + _dot(b_ref[...].astype(BF16), wb_ref[...])
    y_ref[...] = _layer_norm(ALPHA * x_ref[...] + out, g_ref[...], beta_ref[...])


def _sample_out_ab(h_s, b_out, x_s, a_ln_g, a_ln_b, a_w_s, a_b_s, w_out, ln_g, ln_b):
    rows = x_s.shape[0]
    hblock = lambda j: pl.BlockSpec((rows, MIX_A), lambda i: (0, j))
    whole = lambda shape: pl.BlockSpec(shape, lambda i: (0,) * len(shape))
    return pl.pallas_call(
        _sample_out_ab_kernel,
        out_shape=(jax.ShapeDtypeStruct((rows, D_MODEL), F32),
                   jax.ShapeDtypeStruct((rows, MIX_A), F32)),
        grid=(1,),
        in_specs=[hblock(0), hblock(1), hblock(2), whole((rows, MIX_B)), whole((rows, D_MODEL)),
                  whole((1, MIX_A)), whole((1, MIX_A)),
                  whole((A_GROUPS, CHUNK, CHUNK)), whole((A_GROUPS, CHUNK)),
                  pl.BlockSpec((MIX_A, D_MODEL), lambda i: (0, 0)),
                  pl.BlockSpec((MIX_B, D_MODEL), lambda i: (1, 0)),
                  whole((1, D_MODEL)), whole((1, D_MODEL))],
        out_specs=(whole((rows, D_MODEL)), whole((rows, MIX_A))),
        compiler_params=_params(("arbitrary",)),
        name="sample_out_ab_ln",
    )(h_s, h_s, h_s, b_out, x_s, a_ln_g, a_ln_b, a_w_s, a_b_s, w_out, w_out, ln_g, ln_b)


def _sample_pool_kernel(count_pos, hist_ref, hp_ref, pooled_ref, new_ref):
    hp = hp_ref[...]
    parts = []
    for g, w in enumerate(C_WINDOWS):
        cols = slice(g * C_GROUP_DIM, (g + 1) * C_GROUP_DIM)
        s = hp[:, :, cols]
        for k in range(1, w):
            s = s + hist_ref[:, C_HIST - k:C_HIST - k + 1, cols]
        parts.append(s * (1.0 / min(count_pos + 1, w)) - hp[:, :, cols])
    pooled_ref[...] = jnp.concatenate(parts, axis=-1)
    new_ref[:, 0:C_HIST - 1, :] = hist_ref[:, 1:C_HIST, :]
    new_ref[:, C_HIST - 1:C_HIST, :] = hp


def _sample_pool(state, hp3, past_len):
    dec_b = state.shape[0]
    hist = pl.BlockSpec((POOL_BB, C_HIST, C_WIDTH), lambda i: (i, 0, 0))
    one = pl.BlockSpec((POOL_BB, 1, C_WIDTH), lambda i: (i, 0, 0))
    return pl.pallas_call(
        functools.partial(_sample_pool_kernel, past_len),
        out_shape=(jax.ShapeDtypeStruct((dec_b, 1, C_WIDTH), F32),
                   jax.ShapeDtypeStruct((dec_b, C_HIST, C_WIDTH), F32)),
        grid=(dec_b // POOL_BB,),
        in_specs=[hist, one],
        out_specs=(one, hist),
        compiler_params=_params(("parallel",)),
        name="sample_pool",
    )(state, hp3)


def _sample_out_c_kernel(pooled_ref, gate_ref, y_ref, wgrp_ref, bgrp_ref, scale_ref, wout_ref,
                         g_ref, beta_ref, o_ref):
    pooled = pooled_ref[...]
    mixed = jnp.concatenate(
        [_group_mix(pooled[:, g * C_GROUP_DIM:(g + 1) * C_GROUP_DIM], g, wgrp_ref, bgrp_ref)
         for g in range(C_GROUPS)], axis=-1)
    z = mixed * scale_ref[...] * _silu(gate_ref[...])
    out = _dot(z.astype(BF16), wout_ref[...])
    o_ref[...] = _layer_norm(ALPHA * y_ref[...] + out, g_ref[...], beta_ref[...])


def _sample_out_c(pooled, h1, y1, w_grp, b_grp, scale, w_out_c, ln_g, ln_b):
    rows = y1.shape[0]
    whole = lambda shape: pl.BlockSpec(shape, lambda i: (0,) * len(shape))
    return pl.pallas_call(
        _sample_out_c_kernel,
        out_shape=jax.ShapeDtypeStruct((rows, D_MODEL), F32),
        grid=(1,),
        in_specs=[whole((rows, C_WIDTH)), pl.BlockSpec((rows, C_WIDTH), lambda i: (0, 1)),
                  whole((rows, D_MODEL)),
                  whole((C_GROUPS, C_GROUP_DIM, C_GROUP_DIM)),
                  whole((1, C_WIDTH)), whole((1, C_WIDTH)), whole((C_WIDTH, D_MODEL)),
                  whole((1, D_MODEL)), whole((1, D_MODEL))],
        out_specs=whole((rows, D_MODEL)),
        compiler_params=_params(("arbitrary",)),
        name="sample_out_c_ln",
    )(pooled, h1, y1, w_grp, b_grp, scale, w_out_c, ln_g, ln_b)


def kernel(x_prompt, x_sample, cache_k, cache_v, state_pool, page_table, ln_g, ln_b, w_in_ab, a_ln_g, a_ln_b, a_w_s, a_b_s, b_lq1, b_lk1, b_lq2, b_lk2, b_subln_g, w_out_ab, w_in_c, c_w_grp, c_b_grp, c_scale, w_out_c):
    batch, seq, _ = x_prompt.shape
    dec_b = x_sample.shape[0]
    n_pages = page_table.shape[1]
    past_len = n_pages * cache_k.shape[2]
    assert DEPTH == 2 and x_sample.shape[1] == 1 and past_len % CHUNK == 0
    assert seq % TQ == 0 and seq % TM_PROJ == 0 and TM_PROJ % CHUNK == 0 and dec_b % POOL_BB == 0

    w_in = w_in_ab[0].astype(BF16)
    w_out = w_out_ab[0].astype(BF16)
    w_in1 = w_in_c[0].astype(BF16)
    w_grp = c_w_grp[0].astype(BF16)
    w_out1 = w_out_c[0].astype(BF16)
    row = lambda v: v.reshape(1, -1)
    lq1, lk1, lq2, lk2 = row(b_lq1[0]), row(b_lk1[0]), row(b_lq2[0]), row(b_lk2[0])
    subln_g = row(b_subln_g[0])
    aln_g, aln_b = row(a_ln_g[0]), row(a_ln_b[0])
    ln_g0, ln_b0, ln_g1, ln_b1 = row(ln_g[0]), row(ln_b[0]), row(ln_g[1]), row(ln_b[1])
    b_grp, scale = row(c_b_grp[0]), row(c_scale[0])

    xp = x_prompt.reshape(batch * seq, D_MODEL)
    xs = x_sample.reshape(dec_b, D_MODEL)
    h_s = _rows_proj(xs, w_in, MIX_A)
    seg = lambda j: h_s[:, j * MIX_A:(j + 1) * MIX_A].reshape(dec_b, B_HEADS, B_K_ROW)
    q3, kn3, vn3, bg3 = seg(3), seg(4), seg(5), seg(6)
    a_out = _proj_a(xp, w_in, aln_g, aln_b, a_w_s[0], a_b_s[0].T)
    q, k, v, kb, vb, sg = _proj_b(xp, w_in)
    b_out, bo3 = _diff_attention(q, kb, vb, sg, lq1, lk1, lq2, lk2, subln_g, batch, seq,
                                 page_table, q3, kn3, vn3, bg3, cache_k, cache_v)

    y1 = _out_ab(a_out, b_out, xp, w_out, ln_g0, ln_b0, TM_PROJ)
    y2, pool_p = _pool_layer(y1, w_in1, w_grp, b_grp, scale, w_out1, ln_g1, ln_b1, batch, seq)

    y1s, cv_s = _sample_out_ab(h_s, bo3.reshape(dec_b, MIX_B), xs, aln_g, aln_b, a_w_s[0], a_b_s[0],
                               w_out, ln_g0, ln_b0)
    h1s = _rows_proj(y1s, w_in1, C_WIDTH // 2)
    pooled3, pool_s = _sample_pool(state_pool[0], h1s[:, :C_WIDTH].reshape(dec_b, 1, C_WIDTH),
                                   past_len)
    y2s = _sample_out_c(pooled3.reshape(dec_b, C_WIDTH), h1s, y1s, w_grp, b_grp, scale, w_out1,
                        ln_g1, ln_b1)

    return (y2.reshape(batch, seq, D_MODEL),
            y2s.reshape(dec_b, 1, D_MODEL),
            k.reshape(1, batch, seq, B_HEADS, B_K_ROW),
            v.reshape(1, batch, seq, B_HEADS, B_V_DIM),
            kn3.reshape(1, dec_b, 1, B_HEADS, B_K_ROW),
            vn3.reshape(1, dec_b, 1, B_HEADS, B_V_DIM),
            cv_s.reshape(1, dec_b, 1, MIX_A),
            pool_p.reshape(1, batch, C_HIST, C_WIDTH),
            pool_s.reshape(1, dec_b, C_HIST, C_WIDTH))
```

```python
import functools
import math

import jax
import jax.numpy as jnp
from jax import lax
from jax.experimental import pallas as pl
from jax.experimental.pallas import tpu as pltpu

D_MODEL = 2048
DEPTH = 2
PAGE_SIZE = 128
MIX_A = D_MODEL // 2
A_GROUPS = 4
A_GROUP_DIM = MIX_A // A_GROUPS
CHUNK = 128
MIX_B = D_MODEL // 2
B_HEADS = 8
B_QK_DIM = 64
B_K_ROW = 2 * B_QK_DIM
B_V_DIM = MIX_B // B_HEADS
QK_W = B_HEADS * 2 * B_QK_DIM
ATTN_SCALE = 1.0 / math.sqrt(B_QK_DIM)
AB_IN = 3 * MIX_A + 2 * QK_W + 2 * MIX_B
C_WIDTH = D_MODEL
C_WINDOWS = (2, 4, 8, 16)
C_GROUPS = len(C_WINDOWS)
C_GROUP_DIM = C_WIDTH // C_GROUPS
C_HIST = max(C_WINDOWS) - 1
ALPHA = (2 * DEPTH) ** 0.25
LN_EPS = 1e-5
NEG = -1e30
LOG2E = math.log2(math.e)
LAM_INIT_0 = 0.8 - 0.6 * math.exp(-0.3 * 0)

V7X_SUBLANES = 8
V7X_LANES = 128
V7X_VMEM_LIMIT_BYTES = 56 * 1024 * 1024

TM_PROJ = 256
TQ = 512
ATTN_RB = 64
HIST_PAD = 16
POOL_BB = 16

BF16 = jnp.bfloat16
F32 = jnp.float32


def _dot(a, b):
    return jnp.dot(a, b, preferred_element_type=F32)


def _dot_nt(a, b):
    return lax.dot_general(a, b, (((1,), (1,)), ((), ())), preferred_element_type=F32)


def _layer_norm(x, g, b):
    mu = jnp.mean(x, axis=-1, keepdims=True)
    xc = x - mu
    var = jnp.mean(xc * xc, axis=-1, keepdims=True)
    return xc * lax.rsqrt(var + LN_EPS) * g + b


def _silu(x):
    return x / (1.0 + jnp.exp(-x))


def _lam(lq1, lk1, lq2, lk2):
    s1 = jnp.sum(lq1 * lk1, axis=-1, keepdims=True)
    s2 = jnp.sum(lq2 * lk2, axis=-1, keepdims=True)
    return jnp.exp(s1) - jnp.exp(s2) + LAM_INIT_0


def _sub_ln(o, g):
    ms = jnp.mean(o * o, axis=-1, keepdims=True)
    return o * lax.rsqrt(ms + LN_EPS) * g * (1.0 - LAM_INIT_0)


def _params(semantics):
    return pltpu.CompilerParams(dimension_semantics=semantics,
                                vmem_limit_bytes=V7X_VMEM_LIMIT_BYTES)


def _resident(shape, index):
    return pl.BlockSpec(shape, lambda *_: index, pipeline_mode=pl.Buffered(1))


def _proj_a_kernel(x_ref, wu_ref, wv_ref, wg_ref, lng_ref, lnb_ref, ws_ref, bst_ref, o_ref):
    xb = x_ref[...].astype(BF16)
    u = _dot(xb, wu_ref[...])
    v = _layer_norm(_dot(xb, wv_ref[...]), lng_ref[...], lnb_ref[...])
    gate = _dot(xb, wg_ref[...])
    ug = u * _silu(gate)
    row = lax.broadcasted_iota(jnp.int32, (CHUNK, CHUNK), 0)
    col = lax.broadcasted_iota(jnp.int32, (CHUNK, CHUNK), 1)
    tril = row >= col
    bst = bst_ref[...]
    for g in range(A_GROUPS):
        w = jnp.where(tril, ws_ref[g], 0.0).astype(BF16)
        cols = slice(g * A_GROUP_DIM, (g + 1) * A_GROUP_DIM)
        for c in range(TM_PROJ // CHUNK):
            rows = slice(c * CHUNK, (c + 1) * CHUNK)
            mixed = _dot(w, v[rows, cols].astype(BF16)) + bst[:, g:g + 1]
            o_ref[rows, cols] = (ug[rows, cols] * mixed).astype(o_ref.dtype)


def _proj_a(x2d, w_in, a_ln_g, a_ln_b, a_w_s, a_b_s_t):
    m = x2d.shape[0]
    wspec = lambda j: _resident((D_MODEL, MIX_A), (0, j))
    return pl.pallas_call(
        _proj_a_kernel,
        out_shape=jax.ShapeDtypeStruct((m, MIX_A), BF16),
        grid=(m // TM_PROJ,),
        in_specs=[
            pl.BlockSpec((TM_PROJ, D_MODEL), lambda i: (i, 0)),
            wspec(0), wspec(1), wspec(2),
            _resident((1, MIX_A), (0, 0)), _resident((1, MIX_A), (0, 0)),
            _resident((A_GROUPS, CHUNK, CHUNK), (0, 0, 0)),
            _resident((CHUNK, A_GROUPS), (0, 0)),
        ],
        out_specs=pl.BlockSpec((TM_PROJ, MIX_A), lambda i: (i, 0)),
        compiler_params=_params(("parallel",)),
        name="proj_a_gmlp",
    )(x2d, w_in, w_in, w_in, a_ln_g, a_ln_b, a_w_s, a_b_s_t)


def _proj_b_kernel(x_ref, wq_ref, wk_ref, wv_ref, wg_ref, q_ref, k_ref, v_ref, kb_ref, vb_ref, sg_ref):
    xb = x_ref[...].astype(BF16)
    q_ref[...] = (_dot(xb, wq_ref[...]) * (ATTN_SCALE * LOG2E)).astype(q_ref.dtype)
    k = _dot(xb, wk_ref[...])
    k_ref[...] = k
    kb_ref[...] = k.astype(kb_ref.dtype)
    v = _dot(xb, wv_ref[...])
    v_ref[...] = v
    vb_ref[...] = v.astype(vb_ref.dtype)
    sg_ref[...] = _silu(_dot(xb, wg_ref[...])).astype(sg_ref.dtype)


def _proj_b(x2d, w_in):
    m = x2d.shape[0]
    first = 3 * MIX_A // QK_W
    wspec = lambda j: _resident((D_MODEL, QK_W), (0, first + j))
    row = lambda dt: jax.ShapeDtypeStruct((m, QK_W), dt)
    ospec = pl.BlockSpec((TM_PROJ, QK_W), lambda i: (i, 0))
    return pl.pallas_call(
        _proj_b_kernel,
        out_shape=(row(BF16), row(F32), row(F32), row(BF16), row(BF16), row(BF16)),
        grid=(m // TM_PROJ,),
        in_specs=[pl.BlockSpec((TM_PROJ, D_MODEL), lambda i: (i, 0)),
                  wspec(0), wspec(1), wspec(2), wspec(3)],
        out_specs=(ospec,) * 6,
        compiler_params=_params(("parallel",)),
        name="proj_b_qkv",
    )(x2d, w_in, w_in, w_in, w_in)


def _prompt_attn_body(qi, lam, q_ref, k_ref, v_ref, sg_ref, g_ref, o_ref,
                      qq_ref, s_ref, p_ref, m_ref, l_ref, alpha_ref, acc_ref):
    q = q_ref[...]
    lane = lax.broadcasted_iota(jnp.int32, q.shape, 1)
    zero = jnp.zeros_like(q)
    qq_ref[0:TQ, :] = jnp.where(lane < B_QK_DIM, q, zero)
    qq_ref[TQ:, :] = jnp.where(lane >= B_QK_DIM, q, zero)
    m_ref[...] = jnp.full(m_ref.shape, NEG, F32)
    l_ref[...] = jnp.zeros(l_ref.shape, F32)
    acc_ref[...] = jnp.zeros(acc_ref.shape, F32)

    def step(c, masked):
        start = pl.multiple_of(c * TQ, TQ)
        s_ref[...] = _dot_nt(qq_ref[...], k_ref[pl.ds(start, TQ), :])

        for r in range(2 * TQ // ATTN_RB):
            rows = slice(r * ATTN_RB, (r + 1) * ATTN_RB)
            tiles = []
            for j in range(TQ // V7X_LANES):
                t = s_ref[rows, j * V7X_LANES:(j + 1) * V7X_LANES]
                if masked:
                    qrow = (r * ATTN_RB) % TQ + lax.broadcasted_iota(jnp.int32, t.shape, 0)
                    kcol = j * V7X_LANES + lax.broadcasted_iota(jnp.int32, t.shape, 1)
                    t = jnp.where(kcol <= qrow, t, NEG)
                tiles.append(t)
            m_old = m_ref[rows, :]
            m_new = jnp.maximum(m_old, jnp.max(functools.reduce(jnp.maximum, tiles),
                                               axis=-1, keepdims=True))
            alpha = jnp.exp2(m_old - m_new)
            ps = [jnp.exp2(t - m_new) for t in tiles]
            l_ref[rows, :] = alpha * l_ref[rows, :] + functools.reduce(jnp.add, ps)
            m_ref[rows, :] = m_new
            alpha_ref[rows, :] = alpha
            for j, pj in enumerate(ps):
                p_ref[rows, j * V7X_LANES:(j + 1) * V7X_LANES] = pj.astype(BF16)
        acc_ref[...] = alpha_ref[...] * acc_ref[...] + _dot(p_ref[...], v_ref[pl.ds(start, TQ), :])

    def body(c, _):
        step(c, False)
        return 0

    lax.fori_loop(0, qi, body, 0)
    step(qi, True)
    l = jnp.sum(l_ref[...], axis=-1, keepdims=True)
    o = acc_ref[0:TQ, :] / l[:TQ] - lam * (acc_ref[TQ:, :] / l[TQ:])
    o_ref[...] = (_sub_ln(o, g_ref[...]) * sg_ref[...].astype(F32)).astype(o_ref.dtype)


def _decode_attn_body(lam, q_ref, kn_ref, vn_ref, bg_ref, g_ref, k_refs, v_refs, o_ref, s_ref):
    n_pages = len(k_refs)
    page_rows = PAGE_SIZE * B_HEADS
    q = q_ref[...] * ATTN_SCALE
    lane = lax.broadcasted_iota(jnp.int32, q.shape, 1)
    qq = jnp.concatenate([jnp.where(lane < B_QK_DIM, q, 0.0),
                          jnp.where(lane >= B_QK_DIM, q, 0.0)], axis=0)
    qqb = qq.astype(BF16)
    rh = lax.broadcasted_iota(jnp.int32, (2 * B_HEADS, page_rows), 0) % B_HEADS
    ch = lax.broadcasted_iota(jnp.int32, (2 * B_HEADS, page_rows), 1) % B_HEADS
    own = rh == ch
    kn2 = jnp.concatenate([kn_ref[...], kn_ref[...]], axis=0)
    vn2 = jnp.concatenate([vn_ref[...], vn_ref[...]], axis=0)
    s_new = jnp.sum(qq * kn2, axis=-1, keepdims=True)
    m = s_new
    for p in range(n_pages):
        kp = k_refs[p][...].reshape(page_rows, B_K_ROW).astype(BF16)
        s = jnp.where(own, _dot_nt(qqb, kp), NEG)
        s_ref[:, p * page_rows:(p + 1) * page_rows] = s
        m = jnp.maximum(m, jnp.max(s, axis=-1, keepdims=True))
    e_new = jnp.exp(s_new - m)
    l = e_new
    acc = e_new * vn2
    for p in range(n_pages):
        e = jnp.exp(s_ref[:, p * page_rows:(p + 1) * page_rows] - m)
        l = l + jnp.sum(e, axis=-1, keepdims=True)
        vp = v_refs[p][...].reshape(page_rows, B_V_DIM).astype(BF16)
        acc = acc + _dot(e.astype(BF16), vp)
    o = acc[:B_HEADS] / l[:B_HEADS] - lam * (acc[B_HEADS:] / l[B_HEADS:])
    o_ref[...] = _sub_ln(o, g_ref[...]) * _silu(bg_ref[...])


def _page_copies(pt_ref, ck_hbm, cv_hbm, kbuf_ref, vbuf_ref, sem_ref, row, slot):
    copies = []
    for p in range(pt_ref.shape[1]):
        page = pt_ref[row, p]
        copies.append(pltpu.make_async_copy(ck_hbm.at[0, page], kbuf_ref.at[slot, p], sem_ref.at[0, slot]))
        copies.append(pltpu.make_async_copy(cv_hbm.at[0, page], vbuf_ref.at[slot, p], sem_ref.at[1, slot]))
    return copies


def _diff_attn_kernel(pt_ref, q_ref, k_ref, v_ref, sg_ref, lamp_ref, g_ref, drow_ref, ck_hbm, cv_hbm,
                      o_ref, do_ref,
                      qq_ref, s_ref, p_ref, m_ref, l_ref, alpha_ref, acc_ref, ds_ref,
                      kbuf_ref, vbuf_ref, sem_ref):
    n_rows, n_pages = pt_ref.shape
    nq = pl.num_programs(2)
    row = (pl.program_id(0) * B_HEADS + pl.program_id(1)) * nq + pl.program_id(2)
    slot = lax.rem(row, 2)
    copies = functools.partial(_page_copies, pt_ref, ck_hbm, cv_hbm, kbuf_ref, vbuf_ref, sem_ref)

    @pl.when(row == 0)
    def _():
        for c in copies(0, 0):
            c.start()

    @pl.when(row + 1 < n_rows)
    def _():
        for c in copies(row + 1, 1 - slot):
            c.start()

    for c in copies(row, slot):
        c.wait()

    lamp = lamp_ref[...]
    lam = _lam(lamp[0:1], lamp[1:2], lamp[2:3], lamp[3:4])
    k_refs = [kbuf_ref.at[slot, p] for p in range(n_pages)]
    v_refs = [vbuf_ref.at[slot, p] for p in range(n_pages)]
    _decode_attn_body(lam, drow_ref.at[0], drow_ref.at[1], drow_ref.at[2], drow_ref.at[3], g_ref,
                      k_refs, v_refs, do_ref, ds_ref)
    _prompt_attn_body(pl.program_id(2), lam, q_ref, k_ref, v_ref, sg_ref, g_ref, o_ref,
                      qq_ref, s_ref, p_ref, m_ref, l_ref, alpha_ref, acc_ref)


def _diff_attention(q, kb, vb, sg, lam_params, subln_g, batch, seq, page_table, drow, cache_k, cache_v):
    nq = seq // TQ
    dec_b, n_pages = page_table.shape
    assert dec_b == batch * B_HEADS * nq
    step = lambda b, h, i: (b * B_HEADS + h) * nq + i
    qspec = pl.BlockSpec((TQ, B_K_ROW), lambda b, h, i, pt: (b * nq + i, h))
    kvspec = pl.BlockSpec((seq, B_K_ROW), lambda b, h, i, pt: (b, h))
    whole = lambda shape: pl.BlockSpec(shape, lambda b, h, i, pt: (0,) * len(shape))
    hbm = pl.BlockSpec(memory_space=pl.ANY)
    page_buf = pltpu.VMEM((2, n_pages, PAGE_SIZE, B_HEADS, B_K_ROW), F32)
    grid_spec = pltpu.PrefetchScalarGridSpec(
        num_scalar_prefetch=1,
        grid=(batch, B_HEADS, nq),
        in_specs=[qspec, kvspec, kvspec, qspec, whole(lam_params.shape), whole((1, B_V_DIM)),
                  pl.BlockSpec((None,) + drow.shape[1:], lambda b, h, i, pt: (step(b, h, i), 0, 0, 0)),
                  hbm, hbm],
        out_specs=(qspec,
                   pl.BlockSpec((None, B_HEADS, B_V_DIM), lambda b, h, i, pt: (step(b, h, i), 0, 0))),
        scratch_shapes=[pltpu.VMEM((2 * TQ, B_K_ROW), BF16),
                        pltpu.VMEM((2 * TQ, TQ), F32),
                        pltpu.VMEM((2 * TQ, TQ), BF16),
                        pltpu.VMEM((2 * TQ, V7X_LANES), F32),
                        pltpu.VMEM((2 * TQ, V7X_LANES), F32),
                        pltpu.VMEM((2 * TQ, V7X_LANES), F32),
                        pltpu.VMEM((2 * TQ, B_V_DIM), F32),
                        pltpu.VMEM((2 * B_HEADS, n_pages * PAGE_SIZE * B_HEADS), F32),
                        page_buf, page_buf,
                        pltpu.SemaphoreType.DMA((2, 2))],
    )
    return pl.pallas_call(
        _diff_attn_kernel,
        out_shape=(jax.ShapeDtypeStruct((batch * seq, MIX_B), BF16),
                   jax.ShapeDtypeStruct((dec_b, B_HEADS, B_V_DIM), F32)),
        grid_spec=grid_spec,
        compiler_params=_params(("arbitrary", "arbitrary", "arbitrary")),
        name="diff_attn",
    )(page_table, q, kb, vb, sg, lam_params, subln_g, drow, cache_k, cache_v)


def _out_ab_kernel(a_ref, b_ref, x_ref, wa_ref, wb_ref, g_ref, beta_ref, o_ref):
    out = _dot(a_ref[...].astype(BF16), wa_ref[...]) + _dot(b_ref[...].astype(BF16), wb_ref[...])
    o_ref[...] = _layer_norm(ALPHA * x_ref[...] + out, g_ref[...], beta_ref[...])


def _out_ab(a_out, b_out, x2d, w_out, ln_g, ln_b, tm):
    m = x2d.shape[0]
    half = pl.BlockSpec((tm, MIX_A), lambda i: (i, 0))
    full = pl.BlockSpec((tm, D_MODEL), lambda i: (i, 0))
    return pl.pallas_call(
        _out_ab_kernel,
        out_shape=jax.ShapeDtypeStruct((m, D_MODEL), F32),
        grid=(m // tm,),
        in_specs=[half, half, full,
                  _resident((MIX_A, D_MODEL), (0, 0)), _resident((MIX_B, D_MODEL), (1, 0)),
                  _resident((1, D_MODEL), (0, 0)), _resident((1, D_MODEL), (0, 0))],
        out_specs=full,
        compiler_params=_params(("parallel",)),
        name="out_ab_ln",
    )(a_out, b_out, x2d, w_out, w_out, ln_g, ln_b)


def _group_mix(pooled_g, g, wgrp_ref, bgrp_ref):
    cols = slice(g * C_GROUP_DIM, (g + 1) * C_GROUP_DIM)
    return _dot(pooled_g.astype(BF16), wgrp_ref[g]) + bgrp_ref[:, cols]


def _pool_layer_kernel(tiles_per_seq, y_ref, whp_ref, wgate_ref, wgrp_ref, bgrp_ref, scale_ref,
                       wout_ref, g_ref, beta_ref, o_ref, pool_ref, ext_ref):
    tm = y_ref.shape[0]
    t = pl.program_id(0) % tiles_per_seq
    y = y_ref[...]
    yb = y.astype(BF16)
    hp = _dot(yb, whp_ref[...])
    gate = _dot(yb, wgate_ref[...])

    @pl.when(t == 0)
    def _():
        ext_ref[0:HIST_PAD, :] = jnp.zeros((HIST_PAD, C_WIDTH), F32)

    @pl.when(t != 0)
    def _():
        ext_ref[0:HIST_PAD, :] = ext_ref[tm:tm + HIST_PAD, :]

    ext_ref[HIST_PAD:, :] = hp
    pos = t * tm + lax.broadcasted_iota(jnp.int32, (tm, 1), 0)
    mixed = []
    for g, w in enumerate(C_WINDOWS):
        cols = slice(g * C_GROUP_DIM, (g + 1) * C_GROUP_DIM)
        s = ext_ref[:, cols]
        shift = 1
        while shift < w:
            s = s + pltpu.roll(s, shift, axis=0)
            shift *= 2
        inv_count = 1.0 / jnp.minimum(pos + 1, w).astype(F32)
        pooled = s[HIST_PAD:] * inv_count - hp[:, cols]
        mixed.append(_group_mix(pooled, g, wgrp_ref, bgrp_ref))
    z = jnp.concatenate(mixed, axis=-1) * scale_ref[...] * _silu(gate)
    out = _dot(z.astype(BF16), wout_ref[...])
    o_ref[...] = _layer_norm(ALPHA * y + out, g_ref[...], beta_ref[...])
    pool_ref[...] = ext_ref[tm + HIST_PAD - C_HIST:tm + HIST_PAD, :]


def _pool_layer(y2d, w_in_c, w_grp, b_grp, scale, w_out_c, ln_g, ln_b, batch, seq):
    m = y2d.shape[0]
    tm = TM_PROJ
    tiles_per_seq = seq // tm
    full = pl.BlockSpec((tm, D_MODEL), lambda i: (i, 0))
    return pl.pallas_call(
        functools.partial(_pool_layer_kernel, tiles_per_seq),
        out_shape=(jax.ShapeDtypeStruct((m, D_MODEL), F32),
                   jax.ShapeDtypeStruct((batch, C_HIST, C_WIDTH), F32)),
        grid=(m // tm,),
        in_specs=[full,
                  _resident((D_MODEL, C_WIDTH), (0, 0)), _resident((D_MODEL, C_WIDTH), (0, 1)),
                  _resident((C_GROUPS, C_GROUP_DIM, C_GROUP_DIM), (0, 0, 0)),
                  _resident((1, C_WIDTH), (0, 0)), _resident((1, C_WIDTH), (0, 0)),
                  _resident((C_WIDTH, D_MODEL), (0, 0)),
                  _resident((1, D_MODEL), (0, 0)), _resident((1, D_MODEL), (0, 0))],
        out_specs=(full,
                   pl.BlockSpec((None, C_HIST, C_WIDTH), lambda i: (i // tiles_per_seq, 0, 0))),
        scratch_shapes=[pltpu.VMEM((tm + HIST_PAD, C_WIDTH), F32)],
        compiler_params=_params(("arbitrary",)),
        name="pool_layer",
    )(y2d, w_in_c, w_in_c, w_grp, b_grp, scale, w_out_c, ln_g, ln_b)


def _rows_proj_kernel(x_ref, w_ref, o_ref):
    o_ref[...] = _dot(x_ref[...].astype(BF16), w_ref[...])


def _rows_proj(x, w, tn):
    rows, k = x.shape
    n = w.shape[1]
    return pl.pallas_call(
        _rows_proj_kernel,
        out_shape=jax.ShapeDtypeStruct((rows, n), F32),
        grid=(n // tn,),
        in_specs=[_resident((rows, k), (0, 0)), pl.BlockSpec((k, tn), lambda j: (0, j))],
        out_specs=pl.BlockSpec((rows, tn), lambda j: (0, j)),
        compiler_params=_params(("parallel",)),
        name="rows_proj",
    )(x, w)


def _sample_out_ab_kernel(u_ref, v_ref, gate_ref, b_ref, x_ref, lng_ref, lnb_ref, ws_ref, bs_ref,
                          wa_ref, wb_ref, g_ref, beta_ref, y_ref, vn_ref):
    vn = _layer_norm(v_ref[...], lng_ref[...], lnb_ref[...])
    vn_ref[...] = vn
    ug = u_ref[...] * _silu(gate_ref[...])
    mixed = jnp.concatenate(
        [ws_ref[g][0:1, 0:1] * vn[:, g * A_GROUP_DIM:(g + 1) * A_GROUP_DIM] + bs_ref[g:g + 1, 0:1]
         for g in range(A_GROUPS)], axis=-1)
    a_out = ug * mixed
    out = _dot(a_out.astype(BF16), wa_ref[...]) + _dot(b_ref[...].astype(BF16), wb_ref[...])
    y_ref[...] = _layer_norm(ALPHA * x_ref[...] + out, g_ref[...], beta_ref[...])


def _sample_out_ab(h_s, b_out, x_s, a_ln_g, a_ln_b, a_w_s, a_b_s, w_out, ln_g, ln_b):
    rows = x_s.shape[0]
    hblock = lambda j: pl.BlockSpec((rows, MIX_A), lambda i: (0, j))
    whole = lambda shape: pl.BlockSpec(shape, lambda i: (0,) * len(shape))
    return pl.pallas_call(
        _sample_out_ab_kernel,
        out_shape=(jax.ShapeDtypeStruct((rows, D_MODEL), F32),
                   jax.ShapeDtypeStruct((rows, MIX_A), F32)),
        grid=(1,),
        in_specs=[hblock(0), hblock(1), hblock(2), whole((rows, MIX_B)), whole((rows, D_MODEL)),
                  whole((1, MIX_A)), whole((1, MIX_A)),
                  whole((A_GROUPS, CHUNK, CHUNK)), whole((A_GROUPS, CHUNK)),
                  pl.BlockSpec((MIX_A, D_MODEL), lambda i: (0, 0)),
                  pl.BlockSpec((MIX_B, D_MODEL), lambda i: (1, 0)),
                  whole((1, D_MODEL)), whole((1, D_MODEL))],
        out_specs=(whole((rows, D_MODEL)), whole((rows, MIX_A))),
        compiler_params=_params(("arbitrary",)),
        name="sample_out_ab_ln",
    )(h_s, h_s, h_s, b_out, x_s, a_ln_g, a_ln_b, a_w_s, a_b_s, w_out, w_out, ln_g, ln_b)


def _sample_pool_kernel(count_pos, hist_ref, hp_ref, pooled_ref, new_ref):
    hp = hp_ref[...]
    parts = []
    for g, w in enumerate(C_WINDOWS):
        cols = slice(g * C_GROUP_DIM, (g + 1) * C_GROUP_DIM)
        s = hp[:, :, cols]
        for k in range(1, w):
            s = s + hist_ref[:, C_HIST - k:C_HIST - k + 1, cols]
        parts.append(s * (1.0 / min(count_pos + 1, w)) - hp[:, :, cols])
    pooled_ref[...] = jnp.concatenate(parts, axis=-1)
    new_ref[:, 0:C_HIST - 1, :] = hist_ref[:, 1:C_HIST, :]
    new_ref[:, C_HIST - 1:C_HIST, :] = hp


def _sample_pool(state, hp3, past_len):
    dec_b = state.shape[0]
    hist = pl.BlockSpec((POOL_BB, C_HIST, C_WIDTH), lambda i: (i, 0, 0))
    one = pl.BlockSpec((POOL_BB, 1, C_WIDTH), lambda i: (i, 0, 0))
    return pl.pallas_call(
        functools.partial(_sample_pool_kernel, past_len),
        out_shape=(jax.ShapeDtypeStruct((dec_b, 1, C_WIDTH), F32),
                   jax.ShapeDtypeStruct((dec_b, C_HIST, C_WIDTH), F32)),
        grid=(dec_b // POOL_BB,),
        in_specs=[hist, one],
        out_specs=(one, hist),
        compiler_params=_params(("parallel",)),
        name="sample_pool",
    )(state, hp3)


def _sample_out_c_kernel(pooled_ref, gate_ref, y_ref, wgrp_ref, bgrp_ref, scale_ref, wout_ref,
                         g_ref, beta_ref, o_ref):
    pooled = pooled_ref[...]
    mixed = jnp.concatenate(
        [_group_mix(pooled[:, g * C_GROUP_DIM:(g + 1) * C_GROUP_DIM], g, wgrp_ref, bgrp_ref)
         for g in range(C_GROUPS)], axis=-1)
    z = mixed * scale_ref[...] * _silu(gate_ref[...])
    out = _dot(z.astype(BF16), wout_ref[...])
    o_ref[...] = _layer_norm(ALPHA * y_ref[...] + out, g_ref[...], beta_ref[...])


def _sample_out_c(pooled, h1, y1, w_grp, b_grp, scale, w_out_c, ln_g, ln_b):
    rows = y1.shape[0]
    whole = lambda shape: pl.BlockSpec(shape, lambda i: (0,) * len(shape))
    return pl.pallas_call(
        _sample_out_c_kernel,
        out_shape=jax.ShapeDtypeStruct((rows, D_MODEL), F32),
        grid=(1,),
        in_specs=[whole((rows, C_WIDTH)), pl.BlockSpec((rows, C_WIDTH), lambda i: (0, 1)),
                  whole((rows, D_MODEL)),
                  whole((C_GROUPS, C_GROUP_DIM, C_GROUP_DIM)),
                  whole((1, C_WIDTH)), whole((1, C_WIDTH)), whole((C_WIDTH, D_MODEL)),
                  whole((1, D_MODEL)), whole((1, D_MODEL))],
        out_specs=whole((rows, D_MODEL)),
        compiler_params=_params(("arbitrary",)),
        name="sample_out_c_ln",
    )(pooled, h1, y1, w_grp, b_grp, scale, w_out_c, ln_g, ln_b)


def kernel(x_prompt, x_sample, cache_k, cache_v, state_pool, page_table, ln_g, ln_b, w_in_ab, a_ln_g, a_ln_b, a_w_s, a_b_s, b_lq1, b_lk1, b_lq2, b_lk2, b_subln_g, w_out_ab, w_in_c, c_w_grp, c_b_grp, c_scale, w_out_c):
    batch, seq, _ = x_prompt.shape
    dec_b = x_sample.shape[0]
    n_pages = page_table.shape[1]
    past_len = n_pages * cache_k.shape[2]
    assert DEPTH == 2 and x_sample.shape[1] == 1 and past_len % CHUNK == 0
    assert seq % TQ == 0 and seq % TM_PROJ == 0 and TM_PROJ % CHUNK == 0 and dec_b % POOL_BB == 0

    w_in = w_in_ab[0].astype(BF16)
    w_out = w_out_ab[0].astype(BF16)
    w_in1 = w_in_c[0].astype(BF16)
    w_grp = c_w_grp[0].astype(BF16)
    w_out1 = w_out_c[0].astype(BF16)
    row = lambda v: v.reshape(1, -1)
    lq1, lk1, lq2, lk2 = row(b_lq1[0]), row(b_lk1[0]), row(b_lq2[0]), row(b_lk2[0])
    subln_g = row(b_subln_g[0])
    aln_g, aln_b = row(a_ln_g[0]), row(a_ln_b[0])
    ln_g0, ln_b0, ln_g1, ln_b1 = row(ln_g[0]), row(ln_b[0]), row(ln_g[1]), row(ln_b[1])
    b_grp, scale = row(c_b_grp[0]), row(c_scale[0])

    xp = x_prompt.reshape(batch * seq, D_MODEL)
    xs = x_sample.reshape(dec_b, D_MODEL)
    h_s = _rows_proj(xs, w_in, MIX_A)
    drow = h_s[:, 3 * MIX_A:].reshape(dec_b, 4, B_HEADS, B_K_ROW)
    kn3, vn3 = drow[:, 1], drow[:, 2]
    a_out = _proj_a(xp, w_in, aln_g, aln_b, a_w_s[0], a_b_s[0].T)
    q, k, v, kb, vb, sg = _proj_b(xp, w_in)
    b_out, bo3 = _diff_attention(q, kb, vb, sg, jnp.concatenate([lq1, lk1, lq2, lk2], axis=0),
                                 subln_g, batch, seq, page_table, drow, cache_k, cache_v)

    y1 = _out_ab(a_out, b_out, xp, w_out, ln_g0, ln_b0, TM_PROJ)
    y2, pool_p = _pool_layer(y1, w_in1, w_grp, b_grp, scale, w_out1, ln_g1, ln_b1, batch, seq)

    y1s, cv_s = _sample_out_ab(h_s, bo3.reshape(dec_b, MIX_B), xs, aln_g, aln_b, a_w_s[0], a_b_s[0],
                               w_out, ln_g0, ln_b0)
    h1s = _rows_proj(y1s, w_in1, C_WIDTH // 2)
    pooled3, pool_s = _sample_pool(state_pool[0], h1s[:, :C_WIDTH].reshape(dec_b, 1, C_WIDTH),
                                   past_len)
    y2s = _sample_out_c(pooled3.reshape(dec_b, C_WIDTH), h1s, y1s, w_grp, b_grp, scale, w_out1,
                        ln_g1, ln_b1)

    return (y2.reshape(batch, seq, D_MODEL),
            y2s.reshape(dec_b, 1, D_MODEL),
            k.reshape(1, batch, seq, B_HEADS, B_K_ROW),
            v.reshape(1, batch, seq, B_HEADS, B_V_DIM),
            kn3.reshape(1, dec_b, 1, B_HEADS, B_K_ROW),
            vn3.reshape(1, dec_b, 1, B_HEADS, B_V_DIM),
            cv_s.reshape(1, dec_b, 1, MIX_A),
            pool_p.reshape(1, batch, C_HIST, C_WIDTH),
            pool_s.reshape(1, dec_b, C_HIST, C_WIDTH))
```

```python
import functools
import math

import jax
import jax.numpy as jnp
from jax import lax
from jax.experimental import pallas as pl
from jax.experimental.pallas import tpu as pltpu

D_MODEL = 2048
DEPTH = 2
PAGE_SIZE = 128
MIX_A = D_MODEL // 2
A_GROUPS = 4
A_GROUP_DIM = MIX_A // A_GROUPS
CHUNK = 128
MIX_B = D_MODEL // 2
B_HEADS = 8
B_QK_DIM = 64
B_K_ROW = 2 * B_QK_DIM
B_V_DIM = MIX_B // B_HEADS
QK_W = B_HEADS * 2 * B_QK_DIM
ATTN_SCALE = 1.0 / math.sqrt(B_QK_DIM)
AB_IN = 3 * MIX_A + 2 * QK_W + 2 * MIX_B
C_WIDTH = D_MODEL
C_WINDOWS = (2, 4, 8, 16)
C_GROUPS = len(C_WINDOWS)
C_GROUP_DIM = C_WIDTH // C_GROUPS
C_HIST = max(C_WINDOWS) - 1
ALPHA = (2 * DEPTH) ** 0.25
LN_EPS = 1e-5
NEG = -1e30
LOG2E = math.log2(math.e)
LAM_INIT_0 = 0.8 - 0.6 * math.exp(-0.3 * 0)

V7X_SUBLANES = 8
V7X_LANES = 128
BF16_SUBLANES = 16
V7X_VMEM_LIMIT_BYTES = 56 * 1024 * 1024

TM_PROJ = 512
TM_POOL = 256
TQ = 512
ATTN_RB = 64
HIST_PAD = 16
POOL_BB = 16

BF16 = jnp.bfloat16
F32 = jnp.float32


def _dot(a, b):
    return jnp.dot(a, b, preferred_element_type=F32)


def _dot_nt(a, b):
    return lax.dot_general(a, b, (((1,), (1,)), ((), ())), preferred_element_type=F32)


def _layer_norm(x, g, b):
    mu = jnp.mean(x, axis=-1, keepdims=True)
    xc = x - mu
    var = jnp.mean(xc * xc, axis=-1, keepdims=True)
    return xc * lax.rsqrt(var + LN_EPS) * g + b


def _silu(x):
    return x / (1.0 + jnp.exp(-x))


def _lam(lq1, lk1, lq2, lk2):
    s1 = jnp.sum(lq1 * lk1, axis=-1, keepdims=True)
    s2 = jnp.sum(lq2 * lk2, axis=-1, keepdims=True)
    return jnp.exp(s1) - jnp.exp(s2) + LAM_INIT_0


def _sub_ln(o, g):
    ms = jnp.mean(o * o, axis=-1, keepdims=True)
    return o * lax.rsqrt(ms + LN_EPS) * g * (1.0 - LAM_INIT_0)


def _params(semantics):
    return pltpu.CompilerParams(dimension_semantics=semantics,
                                vmem_limit_bytes=V7X_VMEM_LIMIT_BYTES)


def _resident(shape, index):
    return pl.BlockSpec(shape, lambda *_: index, pipeline_mode=pl.Buffered(1))


def _proj_a_kernel(n_cast, x_ref, wu_ref, wv_ref, wg_ref, lng_ref, lnb_ref, ws_ref, bst_ref, *rest):
    cast_in, o_ref, cast_out = rest[:n_cast], rest[n_cast], rest[n_cast + 1:]
    for src, dst in zip(cast_in, cast_out):
        dst[...] = src[...].astype(dst.dtype)
    xb = x_ref[...].astype(BF16)
    u = _dot(xb, wu_ref[...])
    v = _layer_norm(_dot(xb, wv_ref[...]), lng_ref[...], lnb_ref[...])
    gate = _dot(xb, wg_ref[...])
    ug = u * _silu(gate)
    row = lax.broadcasted_iota(jnp.int32, (CHUNK, CHUNK), 0)
    col = lax.broadcasted_iota(jnp.int32, (CHUNK, CHUNK), 1)
    tril = row >= col
    bst = bst_ref[...]
    for g in range(A_GROUPS):
        w = jnp.where(tril, ws_ref[g], 0.0).astype(BF16)
        cols = slice(g * A_GROUP_DIM, (g + 1) * A_GROUP_DIM)
        for c in range(x_ref.shape[0] // CHUNK):
            rows = slice(c * CHUNK, (c + 1) * CHUNK)
            mixed = _dot(w, v[rows, cols].astype(BF16)) + bst[:, g:g + 1]
            o_ref[rows, cols] = (ug[rows, cols] * mixed).astype(o_ref.dtype)


def _proj_a(x2d, w_in, a_ln_g, a_ln_b, a_w_s, a_b_s_t, to_cast):
    m = x2d.shape[0]
    steps = m // TM_PROJ
    wspec = lambda j: _resident((D_MODEL, MIX_A), (0, j))
    slab = lambda w: pl.BlockSpec((w.shape[0] // steps, w.shape[1]), lambda i: (i, 0))
    assert all(w.shape[0] % (steps * BF16_SUBLANES) == 0 for w in to_cast)
    return pl.pallas_call(
        functools.partial(_proj_a_kernel, len(to_cast)),
        out_shape=(jax.ShapeDtypeStruct((m, MIX_A), BF16),
                   *[jax.ShapeDtypeStruct(w.shape, BF16) for w in to_cast]),
        grid=(steps,),
        in_specs=[
            pl.BlockSpec((TM_PROJ, D_MODEL), lambda i: (i, 0)),
            wspec(0), wspec(1), wspec(2),
            _resident((1, MIX_A), (0, 0)), _resident((1, MIX_A), (0, 0)),
            _resident((A_GROUPS, CHUNK, CHUNK), (0, 0, 0)),
            _resident((CHUNK, A_GROUPS), (0, 0)),
            *[slab(w) for w in to_cast],
        ],
        out_specs=(pl.BlockSpec((TM_PROJ, MIX_A), lambda i: (i, 0)), *[slab(w) for w in to_cast]),
        compiler_params=_params(("parallel",)),
        name="proj_a_gmlp",
    )(x2d, w_in, w_in, w_in, a_ln_g, a_ln_b, a_w_s, a_b_s_t, *to_cast)


def _proj_b_kernel(x_ref, wq_ref, wk_ref, wv_ref, wg_ref, q_ref, k_ref, v_ref, kb_ref, vb_ref, sg_ref):
    xb = x_ref[...].astype(BF16)
    q_ref[...] = (_dot(xb, wq_ref[...]) * (ATTN_SCALE * LOG2E)).astype(q_ref.dtype)
    k = _dot(xb, wk_ref[...])
    k_ref[...] = k
    kb_ref[...] = k.astype(kb_ref.dtype)
    v = _dot(xb, wv_ref[...])
    v_ref[...] = v
    vb_ref[...] = v.astype(vb_ref.dtype)
    sg_ref[...] = _silu(_dot(xb, wg_ref[...])).astype(sg_ref.dtype)


def _proj_b(x2d, w_in):
    m = x2d.shape[0]
    first = 3 * MIX_A // QK_W
    wspec = lambda j: _resident((D_MODEL, QK_W), (0, first + j))
    row = lambda dt: jax.ShapeDtypeStruct((m, QK_W), dt)
    ospec = pl.BlockSpec((TM_PROJ, QK_W), lambda i: (i, 0))
    return pl.pallas_call(
        _proj_b_kernel,
        out_shape=(row(BF16), row(F32), row(F32), row(BF16), row(BF16), row(BF16)),
        grid=(m // TM_PROJ,),
        in_specs=[pl.BlockSpec((TM_PROJ, D_MODEL), lambda i: (i, 0)),
                  wspec(0), wspec(1), wspec(2), wspec(3)],
        out_specs=(ospec,) * 6,
        compiler_params=_params(("parallel",)),
        name="proj_b_qkv",
    )(x2d, w_in, w_in, w_in, w_in)


def _prompt_attn_body(qi, lam, q_ref, k_ref, v_ref, sg_ref, g_ref, o_ref,
                      qq_ref, s_ref, p_ref, m_ref, l_ref, alpha_ref, acc_ref):
    q = q_ref[...]
    lane = lax.broadcasted_iota(jnp.int32, q.shape, 1)
    zero = jnp.zeros_like(q)
    qq_ref[0:TQ, :] = jnp.where(lane < B_QK_DIM, q, zero)
    qq_ref[TQ:, :] = jnp.where(lane >= B_QK_DIM, q, zero)
    m_ref[...] = jnp.full(m_ref.shape, NEG, F32)
    l_ref[...] = jnp.zeros(l_ref.shape, F32)
    acc_ref[...] = jnp.zeros(acc_ref.shape, F32)

    def step(c, masked):
        start = pl.multiple_of(c * TQ, TQ)
        s_ref[...] = _dot_nt(qq_ref[...], k_ref[pl.ds(start, TQ), :])

        for r in range(2 * TQ // ATTN_RB):
            rows = slice(r * ATTN_RB, (r + 1) * ATTN_RB)
            tiles = []
            for j in range(TQ // V7X_LANES):
                t = s_ref[rows, j * V7X_LANES:(j + 1) * V7X_LANES]
                if masked:
                    qrow = (r * ATTN_RB) % TQ + lax.broadcasted_iota(jnp.int32, t.shape, 0)
                    kcol = j * V7X_LANES + lax.broadcasted_iota(jnp.int32, t.shape, 1)
                    t = jnp.where(kcol <= qrow, t, NEG)
                tiles.append(t)
            m_old = m_ref[rows, :]
            m_new = jnp.maximum(m_old, jnp.max(functools.reduce(jnp.maximum, tiles),
                                               axis=-1, keepdims=True))
            alpha = jnp.exp2(m_old - m_new)
            ps = [jnp.exp2(t - m_new) for t in tiles]
            l_ref[rows, :] = alpha * l_ref[rows, :] + functools.reduce(jnp.add, ps)
            m_ref[rows, :] = m_new
            alpha_ref[rows, :] = alpha
            for j, pj in enumerate(ps):
                p_ref[rows, j * V7X_LANES:(j + 1) * V7X_LANES] = pj.astype(BF16)
        acc_ref[...] = alpha_ref[...] * acc_ref[...] + _dot(p_ref[...], v_ref[pl.ds(start, TQ), :])

    def body(c, _):
        step(c, False)
        return 0

    lax.fori_loop(0, qi, body, 0)
    step(qi, True)
    l = jnp.sum(l_ref[...], axis=-1, keepdims=True)
    o = acc_ref[0:TQ, :] / l[:TQ] - lam * (acc_ref[TQ:, :] / l[TQ:])
    o_ref[...] = (_sub_ln(o, g_ref[...]) * sg_ref[...].astype(F32)).astype(o_ref.dtype)


def _decode_attn_body(lam, q_ref, kn_ref, vn_ref, bg_ref, g_ref, k_refs, v_refs, o_ref, s_ref):
    n_pages = len(k_refs)
    page_rows = PAGE_SIZE * B_HEADS
    q = q_ref[...] * ATTN_SCALE
    lane = lax.broadcasted_iota(jnp.int32, q.shape, 1)
    qq = jnp.concatenate([jnp.where(lane < B_QK_DIM, q, 0.0),
                          jnp.where(lane >= B_QK_DIM, q, 0.0)], axis=0)
    qqb = qq.astype(BF16)
    rh = lax.broadcasted_iota(jnp.int32, (2 * B_HEADS, page_rows), 0) % B_HEADS
    ch = lax.broadcasted_iota(jnp.int32, (2 * B_HEADS, page_rows), 1) % B_HEADS
    own = rh == ch
    kn2 = jnp.concatenate([kn_ref[...], kn_ref[...]], axis=0)
    vn2 = jnp.concatenate([vn_ref[...], vn_ref[...]], axis=0)
    s_new = jnp.sum(qq * kn2, axis=-1, keepdims=True)
    m = s_new
    for p in range(n_pages):
        kp = k_refs[p][...].reshape(page_rows, B_K_ROW).astype(BF16)
        s = jnp.where(own, _dot_nt(qqb, kp), NEG)
        s_ref[:, p * page_rows:(p + 1) * page_rows] = s
        m = jnp.maximum(m, jnp.max(s, axis=-1, keepdims=True))
    e_new = jnp.exp(s_new - m)
    l = e_new
    acc = e_new * vn2
    for p in range(n_pages):
        e = jnp.exp(s_ref[:, p * page_rows:(p + 1) * page_rows] - m)
        l = l + jnp.sum(e, axis=-1, keepdims=True)
        vp = v_refs[p][...].reshape(page_rows, B_V_DIM).astype(BF16)
        acc = acc + _dot(e.astype(BF16), vp)
    o = acc[:B_HEADS] / l[:B_HEADS] - lam * (acc[B_HEADS:] / l[B_HEADS:])
    o_ref[...] = _sub_ln(o, g_ref[...]) * _silu(bg_ref[...])


def _page_copies(pt_ref, ck_hbm, cv_hbm, kbuf_ref, vbuf_ref, sem_ref, row, slot):
    copies = []
    for p in range(pt_ref.shape[1]):
        page = pt_ref[row, p]
        copies.append(pltpu.make_async_copy(ck_hbm.at[0, page], kbuf_ref.at[slot, p], sem_ref.at[0, slot]))
        copies.append(pltpu.make_async_copy(cv_hbm.at[0, page], vbuf_ref.at[slot, p], sem_ref.at[1, slot]))
    return copies


def _diff_attn_kernel(pt_ref, q_ref, k_ref, v_ref, sg_ref, lamp_ref, g_ref, drow_ref, ck_hbm, cv_hbm,
                      o_ref, do_ref,
                      qq_ref, s_ref, p_ref, m_ref, l_ref, alpha_ref, acc_ref, ds_ref,
                      kbuf_ref, vbuf_ref, sem_ref):
    n_rows, n_pages = pt_ref.shape
    nq = pl.num_programs(2)
    row = (pl.program_id(0) * B_HEADS + pl.program_id(1)) * nq + pl.program_id(2)
    slot = lax.rem(row, 2)
    copies = functools.partial(_page_copies, pt_ref, ck_hbm, cv_hbm, kbuf_ref, vbuf_ref, sem_ref)

    @pl.when(row == 0)
    def _():
        for c in copies(0, 0):
            c.start()

    @pl.when(row + 1 < n_rows)
    def _():
        for c in copies(row + 1, 1 - slot):
            c.start()

    for c in copies(row, slot):
        c.wait()

    lamp = lamp_ref[...]
    lam = _lam(lamp[0:1], lamp[1:2], lamp[2:3], lamp[3:4])
    k_refs = [kbuf_ref.at[slot, p] for p in range(n_pages)]
    v_refs = [vbuf_ref.at[slot, p] for p in range(n_pages)]
    _decode_attn_body(lam, drow_ref.at[0], drow_ref.at[1], drow_ref.at[2], drow_ref.at[3], g_ref,
                      k_refs, v_refs, do_ref, ds_ref)
    _prompt_attn_body(pl.program_id(2), lam, q_ref, k_ref, v_ref, sg_ref, g_ref, o_ref,
                      qq_ref, s_ref, p_ref, m_ref, l_ref, alpha_ref, acc_ref)


def _diff_attention(q, kb, vb, sg, lam_params, subln_g, batch, seq, page_table, drow, cache_k, cache_v):
    nq = seq // TQ
    dec_b, n_pages = page_table.shape
    assert dec_b == batch * B_HEADS * nq
    step = lambda b, h, i: (b * B_HEADS + h) * nq + i
    qspec = pl.BlockSpec((TQ, B_K_ROW), lambda b, h, i, pt: (b * nq + i, h))
    kvspec = pl.BlockSpec((seq, B_K_ROW), lambda b, h, i, pt: (b, h))
    whole = lambda shape: pl.BlockSpec(shape, lambda b, h, i, pt: (0,) * len(shape))
    hbm = pl.BlockSpec(memory_space=pl.ANY)
    page_buf = pltpu.VMEM((2, n_pages, PAGE_SIZE, B_HEADS, B_K_ROW), F32)
    grid_spec = pltpu.PrefetchScalarGridSpec(
        num_scalar_prefetch=1,
        grid=(batch, B_HEADS, nq),
        in_specs=[qspec, kvspec, kvspec, qspec, whole(lam_params.shape), whole((1, B_V_DIM)),
                  pl.BlockSpec((None,) + drow.shape[1:], lambda b, h, i, pt: (step(b, h, i), 0, 0, 0)),
                  hbm, hbm],
        out_specs=(qspec,
                   pl.BlockSpec((None, B_HEADS, B_V_DIM), lambda b, h, i, pt: (step(b, h, i), 0, 0))),
        scratch_shapes=[pltpu.VMEM((2 * TQ, B_K_ROW), BF16),
                        pltpu.VMEM((2 * TQ, TQ), F32),
                        pltpu.VMEM((2 * TQ, TQ), BF16),
                        pltpu.VMEM((2 * TQ, V7X_LANES), F32),
                        pltpu.VMEM((2 * TQ, V7X_LANES), F32),
                        pltpu.VMEM((2 * TQ, V7X_LANES), F32),
                        pltpu.VMEM((2 * TQ, B_V_DIM), F32),
                        pltpu.VMEM((2 * B_HEADS, n_pages * PAGE_SIZE * B_HEADS), F32),
                        page_buf, page_buf,
                        pltpu.SemaphoreType.DMA((2, 2))],
    )
    return pl.pallas_call(
        _diff_attn_kernel,
        out_shape=(jax.ShapeDtypeStruct((batch * seq, MIX_B), BF16),
                   jax.ShapeDtypeStruct((dec_b, B_HEADS, B_V_DIM), F32)),
        grid_spec=grid_spec,
        compiler_params=_params(("arbitrary", "arbitrary", "arbitrary")),
        name="diff_attn",
    )(page_table, q, kb, vb, sg, lam_params, subln_g, drow, cache_k, cache_v)


def _out_ab_kernel(a_ref, b_ref, x_ref, wa_ref, wb_ref, g_ref, beta_ref, o_ref):
    out = _dot(a_ref[...].astype(BF16), wa_ref[...]) + _dot(b_ref[...].astype(BF16), wb_ref[...])
    o_ref[...] = _layer_norm(ALPHA * x_ref[...] + out, g_ref[...], beta_ref[...])


def _out_ab(a_out, b_out, x2d, w_out, ln_g, ln_b, tm):
    m = x2d.shape[0]
    half = pl.BlockSpec((tm, MIX_A), lambda i: (i, 0))
    full = pl.BlockSpec((tm, D_MODEL), lambda i: (i, 0))
    return pl.pallas_call(
        _out_ab_kernel,
        out_shape=jax.ShapeDtypeStruct((m, D_MODEL), F32),
        grid=(m // tm,),
        in_specs=[half, half, full,
                  _resident((MIX_A, D_MODEL), (0, 0)), _resident((MIX_B, D_MODEL), (1, 0)),
                  _resident((1, D_MODEL), (0, 0)), _resident((1, D_MODEL), (0, 0))],
        out_specs=full,
        compiler_params=_params(("parallel",)),
        name="out_ab_ln",
    )(a_out, b_out, x2d, w_out, w_out, ln_g, ln_b)


def _group_mix(pooled_g, g, wgrp_ref, bgrp_ref):
    cols = slice(g * C_GROUP_DIM, (g + 1) * C_GROUP_DIM)
    return _dot(pooled_g.astype(BF16), wgrp_ref[g]) + bgrp_ref[:, cols]


def _pool_layer_kernel(tiles_per_seq, y_ref, whp_ref, wgate_ref, wgrp_ref, bgrp_ref, scale_ref,
                       wout_ref, g_ref, beta_ref, o_ref, pool_ref, ext_ref):
    tm = y_ref.shape[0]
    t = pl.program_id(0) % tiles_per_seq
    y = y_ref[...]
    yb = y.astype(BF16)
    hp = _dot(yb, whp_ref[...])
    gate = _dot(yb, wgate_ref[...])

    @pl.when(t == 0)
    def _():
        ext_ref[0:HIST_PAD, :] = jnp.zeros((HIST_PAD, C_WIDTH), F32)

    @pl.when(t != 0)
    def _():
        ext_ref[0:HIST_PAD, :] = ext_ref[tm:tm + HIST_PAD, :]

    ext_ref[HIST_PAD:, :] = hp
    pos = t * tm + lax.broadcasted_iota(jnp.int32, (tm, 1), 0)
    mixed = []
    for g, w in enumerate(C_WINDOWS):
        cols = slice(g * C_GROUP_DIM, (g + 1) * C_GROUP_DIM)
        s = ext_ref[:, cols]
        shift = 1
        while shift < w:
            s = s + pltpu.roll(s, shift, axis=0)
            shift *= 2
        inv_count = 1.0 / jnp.minimum(pos + 1, w).astype(F32)
        pooled = s[HIST_PAD:] * inv_count - hp[:, cols]
        mixed.append(_group_mix(pooled, g, wgrp_ref, bgrp_ref))
    z = jnp.concatenate(mixed, axis=-1) * scale_ref[...] * _silu(gate)
    out = _dot(z.astype(BF16), wout_ref[...])
    o_ref[...] = _layer_norm(ALPHA * y + out, g_ref[...], beta_ref[...])
    pool_ref[...] = ext_ref[tm + HIST_PAD - C_HIST:tm + HIST_PAD, :]


def _pool_layer(y2d, w_in_c, w_grp, b_grp, scale, w_out_c, ln_g, ln_b, batch, seq):
    m = y2d.shape[0]
    tm = TM_POOL
    tiles_per_seq = seq // tm
    full = pl.BlockSpec((tm, D_MODEL), lambda i: (i, 0))
    return pl.pallas_call(
        functools.partial(_pool_layer_kernel, tiles_per_seq),
        out_shape=(jax.ShapeDtypeStruct((m, D_MODEL), F32),
                   jax.ShapeDtypeStruct((batch, C_HIST, C_WIDTH), F32)),
        grid=(m // tm,),
        in_specs=[full,
                  _resident((D_MODEL, C_WIDTH), (0, 0)), _resident((D_MODEL, C_WIDTH), (0, 1)),
                  _resident((C_GROUPS, C_GROUP_DIM, C_GROUP_DIM), (0, 0, 0)),
                  _resident((1, C_WIDTH), (0, 0)), _resident((1, C_WIDTH), (0, 0)),
                  _resident((C_WIDTH, D_MODEL), (0, 0)),
                  _resident((1, D_MODEL), (0, 0)), _resident((1, D_MODEL), (0, 0))],
        out_specs=(full,
                   pl.BlockSpec((None, C_HIST, C_WIDTH), lambda i: (i // tiles_per_seq, 0, 0))),
        scratch_shapes=[pltpu.VMEM((tm + HIST_PAD, C_WIDTH), F32)],
        compiler_params=_params(("arbitrary",)),
        name="pool_layer",
    )(y2d, w_in_c, w_in_c, w_grp, b_grp, scale, w_out_c, ln_g, ln_b)


def _rows_proj_kernel(x_ref, w_ref, o_ref):
    o_ref[...] = _dot(x_ref[...].astype(BF16), w_ref[...])


def _rows_proj(x, w, tn):
    rows, k = x.shape
    n = w.shape[1]
    return pl.pallas_call(
        _rows_proj_kernel,
        out_shape=jax.ShapeDtypeStruct((rows, n), F32),
        grid=(n // tn,),
        in_specs=[_resident((rows, k), (0, 0)), pl.BlockSpec((k, tn), lambda j: (0, j))],
        out_specs=pl.BlockSpec((rows, tn), lambda j: (0, j)),
        compiler_params=_params(("parallel",)),
        name="rows_proj",
    )(x, w)


def _sample_out_ab_kernel(u_ref, v_ref, gate_ref, b_ref, x_ref, lng_ref, lnb_ref, ws_ref, bs_ref,
                          wa_ref, wb_ref, g_ref, beta_ref, y_ref, vn_ref):
    vn = _layer_norm(v_ref[...], lng_ref[...], lnb_ref[...])
    vn_ref[...] = vn
    ug = u_ref[...] * _silu(gate_ref[...])
    mixed = jnp.concatenate(
        [ws_ref[g][0:1, 0:1] * vn[:, g * A_GROUP_DIM:(g + 1) * A_GROUP_DIM] + bs_ref[g:g + 1, 0:1]
         for g in range(A_GROUPS)], axis=-1)
    a_out = ug * mixed
    out = _dot(a_out.astype(BF16), wa_ref[...]) + _dot(b_ref[...].astype(BF16), wb_ref[...])
    y_ref[...] = _layer_norm(ALPHA * x_ref[...] + out, g_ref[...], beta_ref[...])


def _sample_out_ab(h_s, b_out, x_s, a_ln_g, a_ln_b, a_w_s, a_b_s, w_out, ln_g, ln_b):
    rows = x_s.shape[0]
    hblock = lambda j: pl.BlockSpec((rows, MIX_A), lambda i: (0, j))
    whole = lambda shape: pl.BlockSpec(shape, lambda i: (0,) * len(shape))
    return pl.pallas_call(
        _sample_out_ab_kernel,
        out_shape=(jax.ShapeDtypeStruct((rows, D_MODEL), F32),
                   jax.ShapeDtypeStruct((rows, MIX_A), F32)),
        grid=(1,),
        in_specs=[hblock(0), hblock(1), hblock(2), whole((rows, MIX_B)), whole((rows, D_MODEL)),
                  whole((1, MIX_A)), whole((1, MIX_A)),
                  whole((A_GROUPS, CHUNK, CHUNK)), whole((A_GROUPS, CHUNK)),
                  pl.BlockSpec((MIX_A, D_MODEL), lambda i: (0, 0)),
                  pl.BlockSpec((MIX_B, D_MODEL), lambda i: (1, 0)),
                  whole((1, D_MODEL)), whole((1, D_MODEL))],
        out_specs=(whole((rows, D_MODEL)), whole((rows, MIX_A))),
        compiler_params=_params(("arbitrary",)),
        name="sample_out_ab_ln",
    )(h_s, h_s, h_s, b_out, x_s, a_ln_g, a_ln_b, a_w_s, a_b_s, w_out, w_out, ln_g, ln_b)


def _sample_pool_kernel(count_pos, hist_ref, hp_ref, pooled_ref, new_ref):
    hp = hp_ref[...]
    parts = []
    for g, w in enumerate(C_WINDOWS):
        cols = slice(g * C_GROUP_DIM, (g + 1) * C_GROUP_DIM)
        s = hp[:, :, cols]
        for k in range(1, w):
            s = s + hist_ref[:, C_HIST - k:C_HIST - k + 1, cols]
        parts.append(s * (1.0 / min(count_pos + 1, w)) - hp[:, :, cols])
    pooled_ref[...] = jnp.concatenate(parts, axis=-1)
    new_ref[:, 0:C_HIST - 1, :] = hist_ref[:, 1:C_HIST, :]
    new_ref[:, C_HIST - 1:C_HIST, :] = hp


def _sample_pool(state, hp3, past_len):
    dec_b = state.shape[0]
    hist = pl.BlockSpec((POOL_BB, C_HIST, C_WIDTH), lambda i: (i, 0, 0))
    one = pl.BlockSpec((POOL_BB, 1, C_WIDTH), lambda i: (i, 0, 0))
    return pl.pallas_call(
        functools.partial(_sample_pool_kernel, past_len),
        out_shape=(jax.ShapeDtypeStruct((dec_b, 1, C_WIDTH), F32),
                   jax.ShapeDtypeStruct((dec_b, C_HIST, C_WIDTH), F32)),
        grid=(dec_b // POOL_BB,),
        in_specs=[hist, one],
        out_specs=(one, hist),
        compiler_params=_params(("parallel",)),
        name="sample_pool",
    )(state, hp3)


def _sample_out_c_kernel(pooled_ref, gate_ref, y_ref, wgrp_ref, bgrp_ref, scale_ref, wout_ref,
                         g_ref, beta_ref, o_ref):
    pooled = pooled_ref[...]
    mixed = jnp.concatenate(
        [_group_mix(pooled[:, g * C_GROUP_DIM:(g + 1) * C_GROUP_DIM], g, wgrp_ref, bgrp_ref)
         for g in range(C_GROUPS)], axis=-1)
    z = mixed * scale_ref[...] * _silu(gate_ref[...])
    out = _dot(z.astype(BF16), wout_ref[...])
    o_ref[...] = _layer_norm(ALPHA * y_ref[...] + out, g_ref[...], beta_ref[...])


def _sample_out_c(pooled, h1, y1, w_grp, b_grp, scale, w_out_c, ln_g, ln_b):
    rows = y1.shape[0]
    whole = lambda shape: pl.BlockSpec(shape, lambda i: (0,) * len(shape))
    return pl.pallas_call(
        _sample_out_c_kernel,
        out_shape=jax.ShapeDtypeStruct((rows, D_MODEL), F32),
        grid=(1,),
        in_specs=[whole((rows, C_WIDTH)), pl.BlockSpec((rows, C_WIDTH), lambda i: (0, 1)),
                  whole((rows, D_MODEL)),
                  whole((C_GROUPS, C_GROUP_DIM, C_GROUP_DIM)),
                  whole((1, C_WIDTH)), whole((1, C_WIDTH)), whole((C_WIDTH, D_MODEL)),
                  whole((1, D_MODEL)), whole((1, D_MODEL))],
        out_specs=whole((rows, D_MODEL)),
        compiler_params=_params(("arbitrary",)),
        name="sample_out_c_ln",
    )(pooled, h1, y1, w_grp, b_grp, scale, w_out_c, ln_g, ln_b)


def kernel(x_prompt, x_sample, cache_k, cache_v, state_pool, page_table, ln_g, ln_b, w_in_ab, a_ln_g, a_ln_b, a_w_s, a_b_s, b_lq1, b_lk1, b_lq2, b_lk2, b_subln_g, w_out_ab, w_in_c, c_w_grp, c_b_grp, c_scale, w_out_c):
    batch, seq, _ = x_prompt.shape
    dec_b = x_sample.shape[0]
    n_pages = page_table.shape[1]
    past_len = n_pages * cache_k.shape[2]
    assert DEPTH == 2 and x_sample.shape[1] == 1 and past_len % CHUNK == 0
    assert seq % TQ == 0 and seq % TM_PROJ == 0 and TM_PROJ % CHUNK == 0 and seq % TM_POOL == 0
    assert dec_b % POOL_BB == 0

    w_in = w_in_ab[0].astype(BF16)
    row = lambda v: v.reshape(1, -1)
    lq1, lk1, lq2, lk2 = row(b_lq1[0]), row(b_lk1[0]), row(b_lq2[0]), row(b_lk2[0])
    subln_g = row(b_subln_g[0])
    aln_g, aln_b = row(a_ln_g[0]), row(a_ln_b[0])
    ln_g0, ln_b0, ln_g1, ln_b1 = row(ln_g[0]), row(ln_b[0]), row(ln_g[1]), row(ln_b[1])
    b_grp, scale = row(c_b_grp[0]), row(c_scale[0])

    xp = x_prompt.reshape(batch * seq, D_MODEL)
    xs = x_sample.reshape(dec_b, D_MODEL)
    h_s = _rows_proj(xs, w_in, MIX_A)
    drow = h_s[:, 3 * MIX_A:].reshape(dec_b, 4, B_HEADS, B_K_ROW)
    kn3, vn3 = drow[:, 1], drow[:, 2]
    a_out, w_out, w_in1, w_grp, w_out1 = _proj_a(
        xp, w_in, aln_g, aln_b, a_w_s[0], a_b_s[0].T,
        [w_out_ab[0], w_in_c[0], c_w_grp[0].reshape(C_WIDTH, C_GROUP_DIM), w_out_c[0]])
    w_grp = w_grp.reshape(C_GROUPS, C_GROUP_DIM, C_GROUP_DIM)
    q, k, v, kb, vb, sg = _proj_b(xp, w_in)
    b_out, bo3 = _diff_attention(q, kb, vb, sg, jnp.concatenate([lq1, lk1, lq2, lk2], axis=0),
                                 subln_g, batch, seq, page_table, drow, cache_k, cache_v)

    y1 = _out_ab(a_out, b_out, xp, w_out, ln_g0, ln_b0, TM_PROJ)
    y2, pool_p = _pool_layer(y1, w_in1, w_grp, b_grp, scale, w_out1, ln_g1, ln_b1, batch, seq)

    y1s, cv_s = _sample_out_ab(h_s, bo3.reshape(dec_b, MIX_B), xs, aln_g, aln_b, a_w_s[0], a_b_s[0],
                               w_out, ln_g0, ln_b0)
    h1s = _rows_proj(y1s, w_in1, C_WIDTH // 2)
    pooled3, pool_s = _sample_pool(state_pool[0], h1s[:, :C_WIDTH].reshape(dec_b, 1, C_WIDTH),
                                   past_len)
    y2s = _sample_out_c(pooled3.reshape(dec_b, C_WIDTH), h1s, y1s, w_grp, b_grp, scale, w_out1,
                        ln_g1, ln_b1)

    return (y2.reshape(batch, seq, D_MODEL),
            y2s.reshape(dec_b, 1, D_MODEL),
            k.reshape(1, batch, seq, B_HEADS, B_K_ROW),
            v.reshape(1, batch, seq, B_HEADS, B_V_DIM),
            kn3.reshape(1, dec_b, 1, B_HEADS, B_K_ROW),
            vn3.reshape(1, dec_b, 1, B_HEADS, B_V_DIM),
            cv_s.reshape(1, dec_b, 1, MIX_A),
            pool_p.reshape(1, batch, C_HIST, C_WIDTH),
            pool_s.reshape(1, dec_b, C_HIST, C_WIDTH))
```
